```python
import jax, jax.numpy as jnp
from jax import lax
import numpy as np

D_MODEL = 1024
BATCH = 8
SEQ = 4096
DEPTH = 4

CHUNK = 64
Q_BLOCK = 128

GDN_HEADS = 4
GDN_HEAD_DIM = 128
GDN_WIDTH = GDN_HEADS * GDN_HEAD_DIM
GDN_CONV = 4
SB_HEADS = 8
SB_HEAD_DIM = 64
SB_WIDTH = SB_HEADS * SB_HEAD_DIM
SC_GROUPS = 8
SC_WIDTH = 512
SC_CONV = 3
N_BRANCH = 3
BRANCH_WIDTH = 512
D_FF = 4 * D_MODEL
EPS = 1e-6

_SIZES = (3 * GDN_WIDTH,
          GDN_WIDTH,
          GDN_HEADS,
          GDN_HEADS,
          3 * SB_WIDTH,
          SC_WIDTH,
          SC_WIDTH,
          SC_WIDTH,
          N_BRANCH * D_MODEL)
IN_PROJ_WIDTH = sum(_SIZES)
SPLIT_POINTS = tuple(int(s) for s in np.cumsum(_SIZES)[:-1])

kernel_name = "hybrid_gdn_stickbreak_shortconv_block"


def rms_norm(x, w):
    x32 = x.astype(jnp.float32)
    y = x32 * lax.rsqrt(jnp.mean(x32 * x32, axis=-1, keepdims=True) + EPS)
    return (y * w.astype(jnp.float32)).astype(x.dtype)


def l2_normalize(x):
    return x * lax.rsqrt(jnp.sum(x * x, axis=-1, keepdims=True) + EPS)


def causal_depthwise_conv(x, w):
    k_len, ch = w.shape
    return lax.conv_general_dilated(
        x, w.astype(x.dtype)[:, None, :], window_strides=(1,), padding=[(k_len - 1, 0)],
        dimension_numbers=("NWC", "WIO", "NWC"), feature_group_count=ch)


def gated_delta_rule(q, k, v, g, beta):
    bsz, seq, heads, dk = q.shape
    dv = v.shape[-1]
    n_chunks = seq // CHUNK

    def chunkify(t):
        return jnp.moveaxis(t.reshape((bsz, n_chunks, CHUNK) + t.shape[2:]), 3, 2)

    q, k, v, g, beta = (chunkify(t) for t in (q, k, v, g, beta))
    g = jnp.cumsum(g, axis=-1)
    tri = jnp.tril(jnp.ones((CHUNK, CHUNK), dtype=bool))
    strict = jnp.tril(jnp.ones((CHUNK, CHUNK), dtype=bool), k=-1)
    decay = jnp.exp(jnp.where(tri, g[..., :, None] - g[..., None, :], -jnp.inf))

    k_beta = k * beta[..., None]
    v_beta = v * beta[..., None]
    a_kk = jnp.where(strict, jnp.einsum("bnhcd,bnhed->bnhce", k_beta, k) * decay, 0.0)
    lhs = a_kk + jnp.eye(CHUNK, dtype=a_kk.dtype)
    rhs = jnp.concatenate([v_beta, k_beta * jnp.exp(g)[..., None]], axis=-1)
    sol = lax.linalg.triangular_solve(lhs, rhs, left_side=True, lower=True, unit_diagonal=True)
    u, w = sol[..., :dv], sol[..., dv:]

    a_qk = jnp.where(tri, jnp.einsum("bnhcd,bnhed->bnhce", q, k) * decay, 0.0)
    q_dec = q * jnp.exp(g)[..., None]
    k_dec = k * jnp.exp(g[..., -1:] - g)[..., None]
    g_last = jnp.exp(g[..., -1])

    def step(state, inp):
        q_c, k_c, u_c, w_c, a_c, gl_c = inp
        v_new = u_c - jnp.einsum("bhcd,bhde->bhce", w_c, state)
        o_c = (jnp.einsum("bhcd,bhde->bhce", q_c, state)
               + jnp.einsum("bhcs,bhse->bhce", a_c, v_new))
        state = state * gl_c[..., None, None] + jnp.einsum("bhcd,bhce->bhde", k_c, v_new)
        return state, o_c

    xs = tuple(jnp.moveaxis(t, 1, 0) for t in (q_dec, k_dec, u, w, a_qk, g_last))
    state0 = jnp.zeros((bsz, heads, dk, dv), jnp.float32)
    _, o = lax.scan(step, state0, xs)
    return o.transpose(1, 0, 3, 2, 4).reshape(bsz, seq, heads, dv)


def gdn_branch(qkv, gate, a, b, conv_w, a_log, dt_bias, norm_w):
    bsz, seq, _ = qkv.shape
    out_dtype = qkv.dtype
    qkv = jax.nn.silu(causal_depthwise_conv(qkv, conv_w)).astype(jnp.float32)
    q, k, v = jnp.split(qkv, 3, axis=-1)
    shp = (bsz, seq, GDN_HEADS, GDN_HEAD_DIM)
    q = l2_normalize(q.reshape(shp)) * (GDN_HEAD_DIM ** -0.5)
    k = l2_normalize(k.reshape(shp))
    v = v.reshape(shp)
    beta = jax.nn.sigmoid(b.astype(jnp.float32))
    g = -jnp.exp(a_log.astype(jnp.float32)) * jax.nn.softplus(a.astype(jnp.float32) + dt_bias.astype(jnp.float32))
    o = gated_delta_rule(q, k, v, g, beta)
    o = o * lax.rsqrt(jnp.mean(o * o, axis=-1, keepdims=True) + EPS) * norm_w.astype(jnp.float32)
    o = o * jax.nn.silu(gate.astype(jnp.float32).reshape(shp))
    return o.reshape(bsz, seq, GDN_WIDTH).astype(out_dtype)


def stick_breaking_branch(qkv):
    bsz, seq, _ = qkv.shape
    q, k, v = jnp.split(qkv, 3, axis=-1)
    shp = (bsz, seq, SB_HEADS, SB_HEAD_DIM)
    q, k, v = q.reshape(shp), k.reshape(shp), v.reshape(shp)
    scale = SB_HEAD_DIM ** -0.5
    outs = []
    for blk in range(seq // Q_BLOCK):
        q0 = blk * Q_BLOCK
        k_len = q0 + Q_BLOCK
        z = jnp.einsum("bqhd,bkhd->bhqk", q[:, q0:k_len], k[:, :k_len]).astype(jnp.float32) * scale
        t_idx = q0 + jnp.arange(Q_BLOCK)
        s_idx = jnp.arange(k_len)
        mask = s_idx[None, :] < t_idx[:, None]
        log_1m = jnp.where(mask, jax.nn.log_sigmoid(-z), 0.0)
        after = lax.cumsum(log_1m, axis=3, reverse=True) - log_1m
        att = jnp.where(mask, jnp.exp(jax.nn.log_sigmoid(z) + after), 0.0)
        outs.append(jnp.einsum("bhqk,bkhd->bqhd", att.astype(v.dtype), v[:, :k_len]))
    return jnp.concatenate(outs, axis=1).reshape(bsz, seq, SB_WIDTH)


def short_conv_branch(xin, gate_b, gate_c, conv_w):
    return gate_b * causal_depthwise_conv(gate_c * xin, conv_w)


def setup_inputs(seed: int = 0) -> dict:
    key = jax.random.key(seed)
    ks = jax.random.split(key, 16)
    f32 = jnp.float32

    def normal(k, shape, scale):
        return jax.random.normal(k, shape, f32) * scale

    def gain(k):
        return 1.0 + 0.02 * jax.random.normal(k, (DEPTH, D_MODEL), f32)

    dt = jnp.exp(jax.random.uniform(ks[5], (DEPTH, GDN_HEADS), f32, np.log(1e-3), np.log(1e-1)))
    return {
        "x": jax.random.normal(ks[0], (BATCH, SEQ, D_MODEL), f32),
        "norm_mix_pre": gain(ks[1]),
        "w_in": normal(ks[2], (DEPTH, D_MODEL, IN_PROJ_WIDTH), D_MODEL ** -0.5),
        "conv_qkv_w": normal(ks[3], (DEPTH, GDN_CONV, 3 * GDN_WIDTH), GDN_CONV ** -0.5),
        "gdn_a_log": jnp.log(jax.random.uniform(ks[4], (DEPTH, GDN_HEADS), f32, 1.0, 16.0)),
        "gdn_dt_bias": dt + jnp.log(-jnp.expm1(-dt)),
        "gdn_norm_w": 1.0 + 0.02 * jax.random.normal(ks[6], (DEPTH, GDN_HEAD_DIM), f32),
        "conv_sc_w": normal(ks[7], (DEPTH, SC_CONV, SC_WIDTH), SC_CONV ** -0.5),
        "w_branch": normal(ks[8], (DEPTH, N_BRANCH, BRANCH_WIDTH, D_MODEL), BRANCH_WIDTH ** -0.5),
        "w_out": normal(ks[9], (DEPTH, D_MODEL, D_MODEL), D_MODEL ** -0.5),
        "norm_mix_post": gain(ks[10]),
        "norm_ffn_pre": gain(ks[11]),
        "w_ff1": normal(ks[12], (DEPTH, D_MODEL, D_FF), D_MODEL ** -0.5),
        "w_ff2": normal(ks[13], (DEPTH, D_FF, D_MODEL), D_FF ** -0.5),
        "norm_ffn_post": gain(ks[14]),
    }


def reference(x, norm_mix_pre, w_in, conv_qkv_w, gdn_a_log, gdn_dt_bias, gdn_norm_w, conv_sc_w,
              w_branch, w_out, norm_mix_post, norm_ffn_pre, w_ff1, w_ff2, norm_ffn_post):
    bsz, seq, _ = x.shape
    for l in range(DEPTH):
        h = rms_norm(x, norm_mix_pre[l])
        proj = h @ w_in[l]
        (gdn_qkv, gdn_gate, gdn_a, gdn_b, sb_qkv, sc_x, sc_b, sc_c, gates) = jnp.split(
            proj, SPLIT_POINTS, axis=-1)
        y_a = gdn_branch(gdn_qkv, gdn_gate, gdn_a, gdn_b, conv_qkv_w[l], gdn_a_log[l],
                         gdn_dt_bias[l], gdn_norm_w[l])
        y_b = stick_breaking_branch(sb_qkv)
        y_c = short_conv_branch(sc_x, sc_b, sc_c, conv_sc_w[l])
        gates = jax.nn.sigmoid(gates.reshape(bsz, seq, N_BRANCH, D_MODEL))
        merged = (gates[:, :, 0] * (y_a @ w_branch[l, 0])
                  + gates[:, :, 1] * (y_b @ w_branch[l, 1])
                  + gates[:, :, 2] * (y_c @ w_branch[l, 2]))
        x = x + rms_norm(merged @ w_out[l], norm_mix_post[l])
        h = rms_norm(x, norm_ffn_pre[l])
        f = jnp.square(jax.nn.relu(h @ w_ff1[l])) @ w_ff2[l]
        x = x + rms_norm(f, norm_ffn_post[l])
    return x
```

```python
import functools

import jax
import jax.numpy as jnp
from jax import lax
from jax.experimental import pallas as pl
from jax.experimental.pallas import tpu as pltpu

F32 = jnp.float32
BF16 = jnp.bfloat16
EPS = 1e-6

LANES = 128
SUBLANES = 8
VMEM_LIMIT = 56 * 1024 * 1024

GDN_HEADS = 4
GDN_HEAD_DIM = 128
GDN_WIDTH = GDN_HEADS * GDN_HEAD_DIM
GDN_CHUNK = 64
SB_HEADS = 8
SB_HEAD_DIM = 64
SB_WIDTH = SB_HEADS * SB_HEAD_DIM
SC_WIDTH = 512
N_BRANCH = 3

COL_GDN_QKV = 0
COL_GDN_GATE = 3 * GDN_WIDTH
COL_SB_Q = COL_GDN_GATE + GDN_WIDTH
COL_SB_K = COL_SB_Q + SB_WIDTH
COL_SB_V = COL_SB_K + SB_WIDTH
COL_SC_X = COL_SB_V + SB_WIDTH
COL_SC_C = COL_SC_X + SC_WIDTH
COL_SC_B = COL_SC_C + SC_WIDTH
COL_GATES = COL_SC_B + SC_WIDTH


def _sigmoid(x):
    return 1.0 / (1.0 + jnp.exp(-x))


def _softplus(x):
    return jnp.maximum(x, 0.0) + jnp.log(1.0 + jnp.exp(-jnp.abs(x)))


def _rms(x, w):
    return x * lax.rsqrt(jnp.mean(x * x, axis=-1, keepdims=True) + EPS) * w


def _dot(a, b):
    return jnp.dot(a, b, preferred_element_type=F32)


def _dot_f32(a, b):
    return jnp.dot(a, b, preferred_element_type=F32, precision=lax.Precision.HIGHEST)


def _dot_nt(a, b):
    return lax.dot_general(a, b, (((1,), (1,)), ((), ())), preferred_element_type=F32)


def _inproj_kernel(x_ref, nw_ref, w_ref, wab_ref, o_ref, oab_ref, h_ref):
    @pl.when(pl.program_id(1) == 0)
    def _():
        hb = _rms(x_ref[...], nw_ref[...]).astype(BF16)
        h_ref[...] = hb
        oab_ref[...] = _dot(hb, wab_ref[...])

    o_ref[...] = _dot(h_ref[...], w_ref[...]).astype(o_ref.dtype)


def _in_proj(x2, nw, w_main, w_ab):
    tokens, d = x2.shape
    n = w_main.shape[1]
    tm = min(1024, tokens)
    tn = min(2048, n)
    return pl.pallas_call(
        _inproj_kernel,
        grid=(tokens // tm, n // tn),
        in_specs=[
            pl.BlockSpec((tm, d), lambda i, j: (i, 0)),
            pl.BlockSpec((1, d), lambda i, j: (0, 0)),
            pl.BlockSpec((d, tn), lambda i, j: (0, j)),
            pl.BlockSpec((d, LANES), lambda i, j: (0, 0)),
        ],
        out_specs=[
            pl.BlockSpec((tm, tn), lambda i, j: (i, j)),
            pl.BlockSpec((tm, LANES), lambda i, j: (i, 0)),
        ],
        out_shape=[
            jax.ShapeDtypeStruct((tokens, n), BF16),
            jax.ShapeDtypeStruct((tokens, LANES), F32),
        ],
        scratch_shapes=[pltpu.VMEM((tm, d), BF16)],
        compiler_params=pltpu.CompilerParams(
            dimension_semantics=("parallel", "arbitrary"), vmem_limit_bytes=VMEM_LIMIT),
        name="in_proj",
    )(x2, nw, w_main, w_ab)


def _gdn_kernel(qkv_ref, halo_ref, gate_ref, ab_ref, convw_ref, alog_ref, dtb_ref, nw_ref,
                o_ref, xbuf, ybuf, gbuf, bbuf, state, *, ts, kconv):
    t = pl.program_id(1)
    c_len = GDN_CHUNK
    dk = GDN_HEAD_DIM

    @pl.when(t == 0)
    def _():
        state[...] = jnp.zeros_like(state)

    halo = jnp.where(t == 0, 0.0, halo_ref[0].astype(F32))
    xbuf[0:SUBLANES, :] = halo
    xbuf[SUBLANES:, :] = qkv_ref[0].astype(F32)
    acc = None
    for i in range(kconv):
        term = convw_ref[i:i + 1, :] * xbuf[pl.ds(SUBLANES - (kconv - 1) + i, ts), :]
        acc = term if acc is None else acc + term
    ybuf[...] = acc * _sigmoid(acc)

    ab = ab_ref[0]
    gbuf[...] = -jnp.exp(alog_ref[...]) * _softplus(ab + dtb_ref[...])
    bbuf[...] = _sigmoid(ab)

    row = lax.broadcasted_iota(jnp.int32, (c_len, c_len), 0)
    col = lax.broadcasted_iota(jnp.int32, (c_len, c_len), 1)
    tri = row >= col
    strict = row > col
    tri_f = tri.astype(F32)
    eye = (row == col).astype(F32)
    nw = nw_ref[...]

    def chunk_body(c, carry):
        r0 = pl.multiple_of(c * c_len, c_len)
        rows = pl.ds(r0, c_len)
        gcum = _dot_f32(tri_f, gbuf[rows, :])
        gcum_t = gcum.T
        beta_all = bbuf[rows, :]
        for h in range(GDN_HEADS):
            lo = h * dk
            q = ybuf[rows, lo:lo + dk]
            k = ybuf[rows, GDN_WIDTH + lo:GDN_WIDTH + lo + dk]
            v = ybuf[rows, 2 * GDN_WIDTH + lo:2 * GDN_WIDTH + lo + dk]
            q = q * lax.rsqrt(jnp.sum(q * q, axis=-1, keepdims=True) + EPS) * (dk ** -0.5)
            k = k * lax.rsqrt(jnp.sum(k * k, axis=-1, keepdims=True) + EPS)
            beta = beta_all[:, GDN_HEADS + h:GDN_HEADS + h + 1]
            gc = gcum[:, h:h + 1]
            gr = gcum_t[h:h + 1, :]
            g_end = gcum[c_len - 1:c_len, h:h + 1]
            decay = jnp.exp(jnp.where(tri, gc - gr, -1e30))
            egc = jnp.exp(gc)
            k_beta = k * beta
            v_beta = v * beta
            kb16 = k.astype(BF16)
            a_kk = jnp.where(strict, _dot_nt(k_beta.astype(BF16), kb16) * decay, 0.0)
            p = -a_kk
            tinv = eye + p
            for _ in range(5):
                p = _dot_f32(p, p)
                tinv = tinv + _dot_f32(tinv, p)
            u = _dot_f32(tinv, v_beta)
            w = _dot_f32(tinv, k_beta * egc)
            a_qk = jnp.where(tri, _dot_nt(q.astype(BF16), kb16) * decay, 0.0)
            q_dec = q * egc
            k_dec = k * jnp.exp(g_end - gc)
            s = state[h]
            s16 = s.astype(BF16)
            v_new = u - _dot(w.astype(BF16), s16)
            vn16 = v_new.astype(BF16)
            o = _dot(q_dec.astype(BF16), s16) + _dot(a_qk.astype(BF16), vn16)
            state[h] = s * jnp.exp(g_end) + _dot(k_dec.T.astype(BF16), vn16)
            o = o * lax.rsqrt(jnp.mean(o * o, axis=-1, keepdims=True) + EPS) * nw
            gate = gate_ref[0, rows, lo:lo + dk].astype(F32)
            o_ref[0, rows, lo:lo + dk] = (o * (gate * _sigmoid(gate))).astype(o_ref.dtype)
        return carry

    lax.fori_loop(0, ts // c_len, chunk_body, 0)


def _gdn(proj3, ab3, conv_w, a_log_pad, dt_bias_pad, norm_w):
    bsz, seq, _ = proj3.shape
    kconv = conv_w.shape[0]
    ts = min(256, seq)
    qkv_w = 3 * GDN_WIDTH
    hb = ts // SUBLANES
    return pl.pallas_call(
        functools.partial(_gdn_kernel, ts=ts, kconv=kconv),
        grid=(bsz, seq // ts),
        in_specs=[
            pl.BlockSpec((1, ts, qkv_w), lambda b, t: (b, t, COL_GDN_QKV // qkv_w)),
            pl.BlockSpec((1, SUBLANES, qkv_w),
                         lambda b, t: (b, jnp.maximum(t * hb - 1, 0), COL_GDN_QKV // qkv_w)),
            pl.BlockSpec((1, ts, GDN_WIDTH), lambda b, t: (b, t, COL_GDN_GATE // GDN_WIDTH)),
            pl.BlockSpec((1, ts, LANES), lambda b, t: (b, t, 0)),
            pl.BlockSpec((kconv, qkv_w), lambda b, t: (0, 0)),
            pl.BlockSpec((1, LANES), lambda b, t: (0, 0)),
            pl.BlockSpec((1, LANES), lambda b, t: (0, 0)),
            pl.BlockSpec((1, GDN_HEAD_DIM), lambda b, t: (0, 0)),
        ],
        out_specs=pl.BlockSpec((1, ts, GDN_WIDTH), lambda b, t: (b, t, 0)),
        out_shape=jax.ShapeDtypeStruct((bsz, seq, GDN_WIDTH), BF16),
        scratch_shapes=[
            pltpu.VMEM((SUBLANES + ts, qkv_w), F32),
            pltpu.VMEM((ts, qkv_w), F32),
            pltpu.VMEM((ts, LANES), F32),
            pltpu.VMEM((ts, LANES), F32),
            pltpu.VMEM((GDN_HEADS, GDN_HEAD_DIM, GDN_HEAD_DIM), F32),
        ],
        compiler_params=pltpu.CompilerParams(
            dimension_semantics=("parallel", "arbitrary"), vmem_limit_bytes=VMEM_LIMIT),
        name="gdn",
    )(proj3, proj3, proj3, ab3, conv_w, a_log_pad, dt_bias_pad, norm_w)


def _sb_kernel(q_ref, k_ref, v_ref, o_ref, *, tq):
    qb = pl.program_id(2)
    tk = tq
    scale = SB_HEAD_DIM ** -0.5
    q = q_ref[0]
    lane = lax.broadcasted_iota(jnp.int32, (tq, 2 * SB_HEAD_DIM), 1)
    first = lane < SB_HEAD_DIM
    zero = jnp.zeros_like(q)
    row = lax.broadcasted_iota(jnp.int32, (tq, tk), 0)
    col = lax.broadcasted_iota(jnp.int32, (tq, tk), 1)
    upper = (row > col).astype(BF16)

    def head(qh):
        def body(i, carry):
            r_run, acc = carry
            kb = qb - i
            k0 = pl.multiple_of(kb * tk, tk)
            k = k_ref[0, pl.ds(k0, tk), :]
            v = v_ref[0, pl.ds(k0, tk), :]
            z = _dot_nt(qh, k) * scale
            mask = (col - row) < (qb - kb) * tq
            lm = jnp.where(mask, -_softplus(z), 0.0)
            hi = lm.astype(BF16)
            r1 = lm - hi.astype(F32)
            mid = r1.astype(BF16)
            lo = (r1 - mid.astype(F32)).astype(BF16)
            after = _dot(hi, upper) + _dot(mid, upper) + _dot(lo, upper)
            att = jnp.where(mask, jnp.exp(z + lm + after + r_run), 0.0)
            acc = acc + _dot(att.astype(BF16), v)
            r_run = r_run + jnp.sum(lm, axis=-1, keepdims=True)
            return r_run, acc

        init = (jnp.zeros((tq, 1), F32), jnp.zeros((tq, 2 * SB_HEAD_DIM), F32))
        return lax.fori_loop(0, qb + 1, body, init)[1]

    o0 = head(jnp.where(first, q, zero))
    o1 = head(jnp.where(first, zero, q))
    o_ref[0] = jnp.where(first, o0, o1).astype(o_ref.dtype)


def _sb(proj3):
    bsz, seq, _ = proj3.shape
    tq = min(256, seq)
    pw = 2 * SB_HEAD_DIM
    pairs = SB_HEADS // 2
    return pl.pallas_call(
        functools.partial(_sb_kernel, tq=tq),
        grid=(bsz, pairs, seq // tq),
        in_specs=[
            pl.BlockSpec((1, tq, pw), lambda b, p, i: (b, i, COL_SB_Q // pw + p)),
            pl.BlockSpec((1, seq, pw), lambda b, p, i: (b, 0, COL_SB_K // pw + p)),
            pl.BlockSpec((1, seq, pw), lambda b, p, i: (b, 0, COL_SB_V // pw + p)),
        ],
        out_specs=pl.BlockSpec((1, tq, pw), lambda b, p, i: (b, i, p)),
        out_shape=jax.ShapeDtypeStruct((bsz, seq, SB_WIDTH), BF16),
        compiler_params=pltpu.CompilerParams(
            dimension_semantics=("parallel", "parallel", "arbitrary"), vmem_limit_bytes=VMEM_LIMIT),
        name="stick_breaking",
    )(proj3, proj3, proj3)


def _merge_kernel(x_ref, ya_ref, yb_ref, scx_ref, scxh_ref, scc_ref, scch_ref, scb_ref,
                  g0_ref, g1_ref, g2_ref, convw_ref, wbr_ref, wout_ref, nw_ref, o_ref, cbuf,
                  *, tm, kconv):
    t = pl.program_id(1)
    halo = scxh_ref[0].astype(F32) * scch_ref[0].astype(F32)
    cbuf[0:SUBLANES, :] = jnp.where(t == 0, 0.0, halo)
    cbuf[SUBLANES:, :] = scx_ref[0].astype(F32) * scc_ref[0].astype(F32)
    conv = None
    for i in range(kconv):
        term = convw_ref[i:i + 1, :] * cbuf[pl.ds(SUBLANES - (kconv - 1) + i, tm), :]
        conv = term if conv is None else conv + term
    yc = scb_ref[0].astype(F32) * conv
    merged = (_sigmoid(g0_ref[0].astype(F32)) * _dot(ya_ref[0], wbr_ref[0])
              + _sigmoid(g1_ref[0].astype(F32)) * _dot(yb_ref[0], wbr_ref[1])
              + _sigmoid(g2_ref[0].astype(F32)) * _dot(yc.astype(BF16), wbr_ref[2]))
    r = _dot(merged.astype(BF16), wout_ref[...])
    o_ref[0] = x_ref[0] + _rms(r, nw_ref[...])


def _merge(x3, ya, yb, proj3, conv_w, w_br, w_out, nw):
    bsz, seq, d = x3.shape
    kconv = conv_w.shape[0]
    tm = min(512, seq)
    hb = tm // SUBLANES
    sw = SC_WIDTH

    def cur(colblk):
        return lambda b, t: (b, t, colblk)

    def prev(colblk):
        return lambda b, t: (b, jnp.maximum(t * hb - 1, 0), colblk)

    const2 = lambda b, t: (0, 0)
    return pl.pallas_call(
        functools.partial(_merge_kernel, tm=tm, kconv=kconv),
        grid=(bsz, seq // tm),
        in_specs=[
            pl.BlockSpec((1, tm, d), cur(0)),
            pl.BlockSpec((1, tm, GDN_WIDTH), cur(0)),
            pl.BlockSpec((1, tm, SB_WIDTH), cur(0)),
            pl.BlockSpec((1, tm, sw), cur(COL_SC_X // sw)),
            pl.BlockSpec((1, SUBLANES, sw), prev(COL_SC_X // sw)),
            pl.BlockSpec((1, tm, sw), cur(COL_SC_C // sw)),
            pl.BlockSpec((1, SUBLANES, sw), prev(COL_SC_C // sw)),
            pl.BlockSpec((1, tm, sw), cur(COL_SC_B // sw)),
            pl.BlockSpec((1, tm, d), cur(COL_GATES // d)),
            pl.BlockSpec((1, tm, d), cur(COL_GATES // d + 1)),
            pl.BlockSpec((1, tm, d), cur(COL_GATES // d + 2)),
            pl.BlockSpec((kconv, sw), const2),
            pl.BlockSpec((N_BRANCH, sw, d), lambda b, t: (0, 0, 0)),
            pl.BlockSpec((d, d), const2),
            pl.BlockSpec((1, d), const2),
        ],
        out_specs=pl.BlockSpec((1, tm, d), cur(0)),
        out_shape=jax.ShapeDtypeStruct((bsz, seq, d), F32),
        scratch_shapes=[pltpu.VMEM((SUBLANES + tm, sw), F32)],
        compiler_params=pltpu.CompilerParams(
            dimension_semantics=("parallel", "arbitrary"), vmem_limit_bytes=VMEM_LIMIT),
        name="merge",
    )(x3, ya, yb, proj3, proj3, proj3, proj3, proj3, proj3, proj3, proj3, conv_w, w_br, w_out, nw)


def _ffn_kernel(x_ref, nw1_ref, w1_ref, w2_ref, nw2_ref, o_ref, acc_ref):
    j = pl.program_id(1)

    @pl.when(j == 0)
    def _():
        acc_ref[...] = jnp.zeros_like(acc_ref)

    x = x_ref[...]
    h = _rms(x, nw1_ref[...]).astype(BF16)
    f = jnp.maximum(_dot(h, w1_ref[...]), 0.0)
    acc_ref[...] += _dot((f * f).astype(BF16), w2_ref[...])

    @pl.when(j == pl.num_programs(1) - 1)
    def _():
        o_ref[...] = x + _rms(acc_ref[...], nw2_ref[...])


def _ffn(x2, nw1, w1, w2, nw2):
    tokens, d = x2.shape
    dff = w1.shape[1]
    tm = min(1024, tokens)
    tf = min(1024, dff)
    return pl.pallas_call(
        _ffn_kernel,
        grid=(tokens // tm, dff // tf),
        in_specs=[
            pl.BlockSpec((tm, d), lambda i, j: (i, 0)),
            pl.BlockSpec((1, d), lambda i, j: (0, 0)),
            pl.BlockSpec((d, tf), lambda i, j: (0, j)),
            pl.BlockSpec((tf, d), lambda i, j: (j, 0)),
            pl.BlockSpec((1, d), lambda i, j: (0, 0)),
        ],
        out_specs=pl.BlockSpec((tm, d), lambda i, j: (i, 0)),
        out_shape=jax.ShapeDtypeStruct((tokens, d), F32),
        scratch_shapes=[pltpu.VMEM((tm, d), F32)],
        compiler_params=pltpu.CompilerParams(
            dimension_semantics=("parallel", "arbitrary"), vmem_limit_bytes=VMEM_LIMIT),
        name="ffn",
    )(x2, nw1, w1, w2, nw2)


def _split_w_in(w):
    gw, sw = GDN_WIDTH, SB_WIDTH
    o = 0
    gdn_qkv = w[:, o:o + 3 * gw]; o += 3 * gw
    gdn_gate = w[:, o:o + gw]; o += gw
    ab = w[:, o:o + 2 * GDN_HEADS]; o += 2 * GDN_HEADS
    sb_qkv = w[:, o:o + 3 * sw]; o += 3 * sw
    sc_x = w[:, o:o + SC_WIDTH]; o += SC_WIDTH
    sc_b = w[:, o:o + SC_WIDTH]; o += SC_WIDTH
    sc_c = w[:, o:o + SC_WIDTH]; o += SC_WIDTH
    gates = w[:, o:]
    main = jnp.concatenate([gdn_qkv, gdn_gate, sb_qkv, sc_x, sc_c, sc_b, gates], axis=1).astype(BF16)
    ab_pad = jnp.pad(ab, ((0, 0), (0, LANES - 2 * GDN_HEADS))).astype(BF16)
    return main, ab_pad


def _pad_lanes(v, offset=0):
    return jnp.pad(v.astype(F32), (offset, LANES - offset - v.shape[0]))[None, :]


def kernel(x, norm_mix_pre, w_in, conv_qkv_w, gdn_a_log, gdn_dt_bias, gdn_norm_w, conv_sc_w,
           w_branch, w_out, norm_mix_post, norm_ffn_pre, w_ff1, w_ff2, norm_ffn_post):
    bsz, seq, d = x.shape
    depth = w_in.shape[0]
    tokens = bsz * seq
    for l in range(depth):
        w_main, w_ab = _split_w_in(w_in[l])
        proj, ab = _in_proj(x.reshape(tokens, d), norm_mix_pre[l][None, :], w_main, w_ab)
        proj3 = proj.reshape(bsz, seq, -1)
        y_a = _gdn(proj3, ab.reshape(bsz, seq, LANES), conv_qkv_w[l], _pad_lanes(gdn_a_log[l]),
                   _pad_lanes(gdn_dt_bias[l]), gdn_norm_w[l][None, :])
        y_b = _sb(proj3)
        x = _merge(x, y_a, y_b, proj3, conv_sc_w[l], w_branch[l].astype(BF16), w_out[l].astype(BF16),
                   norm_mix_post[l][None, :])
        x = _ffn(x.reshape(tokens, d), norm_ffn_pre[l][None, :], w_ff1[l].astype(BF16),
                 w_ff2[l].astype(BF16), norm_ffn_post[l][None, :]).reshape(bsz, seq, d)
    return x
```

```python
import functools

import jax
import jax.numpy as jnp
from jax import lax
from jax.experimental import pallas as pl
from jax.experimental.pallas import tpu as pltpu

F32 = jnp.float32
BF16 = jnp.bfloat16
EPS = 1e-6

LANES = 128
SUBLANES = 8
VMEM_LIMIT = 56 * 1024 * 1024

GDN_HEADS = 4
GDN_HEAD_DIM = 128
GDN_WIDTH = GDN_HEADS * GDN_HEAD_DIM
GDN_CHUNK = 64
SB_HEADS = 8
SB_HEAD_DIM = 64
SB_WIDTH = SB_HEADS * SB_HEAD_DIM
SC_WIDTH = 512
N_BRANCH = 3

COL_GDN_QKV = 0
COL_GDN_GATE = 3 * GDN_WIDTH
COL_SB_Q = COL_GDN_GATE + GDN_WIDTH
COL_SB_K = COL_SB_Q + SB_WIDTH
COL_SB_V = COL_SB_K + SB_WIDTH
COL_SC_X = COL_SB_V + SB_WIDTH
COL_SC_C = COL_SC_X + SC_WIDTH
COL_SC_B = COL_SC_C + SC_WIDTH
COL_GATES = COL_SC_B + SC_WIDTH


def _sigmoid(x):
    return 1.0 / (1.0 + jnp.exp(-x))


def _softplus(x):
    return jnp.maximum(x, 0.0) + jnp.log(1.0 + jnp.exp(-jnp.abs(x)))


def _rms(x, w):
    return x * lax.rsqrt(jnp.mean(x * x, axis=-1, keepdims=True) + EPS) * w


def _dot(a, b):
    return jnp.dot(a, b, preferred_element_type=F32)


def _dot_nt(a, b):
    return lax.dot_general(a, b, (((1,), (1,)), ((), ())), preferred_element_type=F32)


def _split2(a):
    hi = a.astype(BF16)
    return hi, (a - hi.astype(F32)).astype(BF16)


def _inproj_kernel(x_ref, nw_ref, w_ref, wab_ref, o_ref, oab_ref, h_ref):
    @pl.when(pl.program_id(1) == 0)
    def _():
        hb = _rms(x_ref[...], nw_ref[...]).astype(BF16)
        h_ref[...] = hb
        oab_ref[...] = _dot(hb, wab_ref[...])

    o_ref[...] = _dot(h_ref[...], w_ref[...]).astype(o_ref.dtype)


def _in_proj(x2, nw, w_main, w_ab):
    tokens, d = x2.shape
    n = w_main.shape[1]
    tm = min(1024, tokens)
    tn = min(2048, n)
    return pl.pallas_call(
        _inproj_kernel,
        grid=(tokens // tm, n // tn),
        in_specs=[
            pl.BlockSpec((tm, d), lambda i, j: (i, 0)),
            pl.BlockSpec((1, d), lambda i, j: (0, 0)),
            pl.BlockSpec((d, tn), lambda i, j: (0, j)),
            pl.BlockSpec((d, LANES), lambda i, j: (0, 0)),
        ],
        out_specs=[
            pl.BlockSpec((tm, tn), lambda i, j: (i, j)),
            pl.BlockSpec((tm, LANES), lambda i, j: (i, 0)),
        ],
        out_shape=[
            jax.ShapeDtypeStruct((tokens, n), BF16),
            jax.ShapeDtypeStruct((tokens, LANES), F32),
        ],
        scratch_shapes=[pltpu.VMEM((tm, d), BF16)],
        compiler_params=pltpu.CompilerParams(
            dimension_semantics=("parallel", "arbitrary"), vmem_limit_bytes=VMEM_LIMIT),
        name="in_proj",
    )(x2, nw, w_main, w_ab)


def _bmm(a, b):
    return jnp.einsum("bmk,bkn->bmn", a, b, preferred_element_type=F32)


def _bmm_nt(a, b):
    return jnp.einsum("bmk,bnk->bmn", a, b, preferred_element_type=F32)


def _bmm3(a, b):
    ah, al = _split2(a)
    bh, bl = _split2(b)
    return _bmm(ah, bh) + (_bmm(ah, bl) + _bmm(al, bh))


def _gdn_kernel(qkv_ref, halo_ref, gate_ref, ab_ref, convw_ref, alog_ref, dtb_ref, nw_ref,
                o_ref, xbuf, state, *, ts, kconv):
    t = pl.program_id(1)
    c_len = GDN_CHUNK
    dk = GDN_HEAD_DIM
    n_ch = ts // c_len

    @pl.when(t == 0)
    def _():
        state[...] = jnp.zeros_like(state)

    xbuf[0:SUBLANES, :] = jnp.where(t == 0, 0.0, halo_ref[0].astype(F32))
    xbuf[SUBLANES:, :] = qkv_ref[0].astype(F32)
    acc = None
    for i in range(kconv):
        term = convw_ref[i:i + 1, :] * xbuf[pl.ds(SUBLANES - (kconv - 1) + i, ts), :]
        acc = term if acc is None else acc + term
    y = acc * _sigmoid(acc)

    ab = ab_ref[0]
    g_all = -jnp.exp(alog_ref[...]) * _softplus(ab + dtb_ref[...])
    beta_all = _sigmoid(ab)

    rt = lax.broadcasted_iota(jnp.int32, (ts, ts), 0)
    ct = lax.broadcasted_iota(jnp.int32, (ts, ts), 1)
    cum_mat = ((rt >= ct) & (rt // c_len == ct // c_len)).astype(BF16)
    g_hi = g_all.astype(BF16)
    g_r32 = g_all - g_hi.astype(F32)
    g_mid = g_r32.astype(BF16)
    g_lo = (g_r32 - g_mid.astype(F32)).astype(BF16)
    gcum = _dot(cum_mat, g_hi) + _dot(cum_mat, g_mid) + _dot(cum_mat, g_lo)

    def per_head(fn):
        return jnp.concatenate([fn(h) for h in range(GDN_HEADS)], axis=0)

    def head_cols(base):
        return per_head(lambda h: y[:, base + h * dk:base + (h + 1) * dk].reshape(n_ch, c_len, dk))

    q = head_cols(0)
    k = head_cols(GDN_WIDTH)
    v = head_cols(2 * GDN_WIDTH)
    q = q * lax.rsqrt(jnp.sum(q * q, axis=-1, keepdims=True) + EPS) * (dk ** -0.5)
    k = k * lax.rsqrt(jnp.sum(k * k, axis=-1, keepdims=True) + EPS)
    gc = per_head(lambda h: gcum[:, h:h + 1].reshape(n_ch, c_len, 1))
    beta = per_head(lambda h: beta_all[:, GDN_HEADS + h:GDN_HEADS + h + 1].reshape(n_ch, c_len, 1))
    gcum_t = [gcum[c * c_len:(c + 1) * c_len, :].T for c in range(n_ch)]
    gr = jnp.stack([gcum_t[c][h:h + 1, :] for h in range(GDN_HEADS) for c in range(n_ch)])
    g_end = gc[:, c_len - 1:c_len, :]

    row = lax.broadcasted_iota(jnp.int32, (c_len, c_len), 0)
    col = lax.broadcasted_iota(jnp.int32, (c_len, c_len), 1)
    tri = (row >= col)[None]
    strict = (row > col)[None]
    eye = (row == col).astype(F32)[None]

    decay = jnp.exp(jnp.where(tri, gc - gr, -1e30))
    egc = jnp.exp(gc)
    k_beta = k * beta
    k16 = k.astype(BF16)
    a_kk = jnp.where(strict, _bmm_nt(k_beta.astype(BF16), k16) * decay, 0.0)
    p = -a_kk
    tinv = eye + p
    for _ in range(5):
        p = _bmm3(p, p)
        tinv = tinv + _bmm3(tinv, p)
    sol = _bmm3(tinv, jnp.concatenate([v * beta, k_beta * egc], axis=-1))
    u = sol[:, :, :dk]
    w16 = sol[:, :, dk:].astype(BF16)
    a_qk16 = jnp.where(tri, _bmm_nt(q.astype(BF16), k16) * decay, 0.0).astype(BF16)
    q_dec16 = (q * egc).astype(BF16)
    k_dec = k * jnp.exp(g_end - gc)
    s_scale = jnp.exp(g_end)

    nw = nw_ref[...]
    s = [state[h] for h in range(GDN_HEADS)]
    for c in range(n_ch):
        rows = slice(c * c_len, (c + 1) * c_len)
        for h in range(GDN_HEADS):
            i = h * n_ch + c
            s16 = s[h].astype(BF16)
            v_new = u[i] - _dot(w16[i], s16)
            vn16 = v_new.astype(BF16)
            o = _dot(q_dec16[i], s16) + _dot(a_qk16[i], vn16)
            s[h] = s[h] * s_scale[i] + _dot(k_dec[i].T.astype(BF16), vn16)
            o = o * lax.rsqrt(jnp.mean(o * o, axis=-1, keepdims=True) + EPS) * nw
            gate = gate_ref[0, rows, h * dk:(h + 1) * dk].astype(F32)
            o_ref[0, rows, h * dk:(h + 1) * dk] = (o * (gate * _sigmoid(gate))).astype(o_ref.dtype)
    for h in range(GDN_HEADS):
        state[h] = s[h]


def _gdn(proj3, ab3, conv_w, a_log_pad, dt_bias_pad, norm_w):
    bsz, seq, _ = proj3.shape
    kconv = conv_w.shape[0]
    ts = min(256, seq)
    qkv_w = 3 * GDN_WIDTH
    hb = ts // SUBLANES
    return pl.pallas_call(
        functools.partial(_gdn_kernel, ts=ts, kconv=kconv),
        grid=(bsz, seq // ts),
        in_specs=[
            pl.BlockSpec((1, ts, qkv_w), lambda b, t: (b, t, COL_GDN_QKV // qkv_w)),
            pl.BlockSpec((1, SUBLANES, qkv_w),
                         lambda b, t: (b, jnp.maximum(t * hb - 1, 0), COL_GDN_QKV // qkv_w)),
            pl.BlockSpec((1, ts, GDN_WIDTH), lambda b, t: (b, t, COL_GDN_GATE // GDN_WIDTH)),
            pl.BlockSpec((1, ts, LANES), lambda b, t: (b, t, 0)),
            pl.BlockSpec((kconv, qkv_w), lambda b, t: (0, 0)),
            pl.BlockSpec((1, LANES), lambda b, t: (0, 0)),
            pl.BlockSpec((1, LANES), lambda b, t: (0, 0)),
            pl.BlockSpec((1, GDN_HEAD_DIM), lambda b, t: (0, 0)),
        ],
        out_specs=pl.BlockSpec((1, ts, GDN_WIDTH), lambda b, t: (b, t, 0)),
        out_shape=jax.ShapeDtypeStruct((bsz, seq, GDN_WIDTH), BF16),
        scratch_shapes=[
            pltpu.VMEM((SUBLANES + ts, qkv_w), F32),
            pltpu.VMEM((GDN_HEADS, GDN_HEAD_DIM, GDN_HEAD_DIM), F32),
        ],
        compiler_params=pltpu.CompilerParams(
            dimension_semantics=("parallel", "arbitrary"), vmem_limit_bytes=VMEM_LIMIT),
        name="gdn",
    )(proj3, proj3, proj3, ab3, conv_w, a_log_pad, dt_bias_pad, norm_w)


SB_EXIT = 104.0


def _sb_kernel(q_ref, k_ref, v_ref, o_ref, *, tq):
    qb = pl.program_id(2)
    pw = 2 * SB_HEAD_DIM
    q = q_ref[0] * jnp.asarray(SB_HEAD_DIM ** -0.5, BF16)
    first = lax.broadcasted_iota(jnp.int32, (tq, pw), 1) < SB_HEAD_DIM
    zero = jnp.zeros_like(q)
    q0 = jnp.where(first, q, zero)
    q1 = jnp.where(first, zero, q)
    row = lax.broadcasted_iota(jnp.int32, (tq, tq), 0)
    col = lax.broadcasted_iota(jnp.int32, (tq, tq), 1)
    upper = (row > col).astype(BF16)
    causal = col < row

    def block(qh, k, v, r_run, acc, mask):
        z = _dot_nt(qh, k)
        lm = -_softplus(z)
        if mask is not None:
            lm = jnp.where(mask, lm, 0.0)
        hi, lo = _split2(lm)
        after = _dot(hi, upper) + _dot(lo, upper)
        att = jnp.exp((z + lm) + after + r_run)
        if mask is not None:
            att = jnp.where(mask, att, 0.0)
        acc = acc + _dot(att.astype(BF16), v)
        return r_run + jnp.sum(lm, axis=-1, keepdims=True), acc

    def kv(kb):
        k0 = pl.multiple_of(kb * tq, tq)
        return k_ref[0, pl.ds(k0, tq), :], v_ref[0, pl.ds(k0, tq), :]

    def live(r0, r1):
        return jnp.max(jnp.maximum(r0, r1))

    zr = jnp.zeros((tq, 1), F32)
    za = jnp.zeros((tq, pw), F32)
    kd, vd = kv(qb)
    r0, a0 = block(q0, kd, vd, zr, za, causal)
    r1, a1 = block(q1, kd, vd, zr, za, causal)

    def cond(c):
        return jnp.logical_and(c[0] <= qb, c[1] > -SB_EXIT)

    def body(c):
        i, _, r0, a0, r1, a1 = c
        k, v = kv(qb - i)
        r0, a0 = block(q0, k, v, r0, a0, None)
        r1, a1 = block(q1, k, v, r1, a1, None)
        return i + 1, live(r0, r1), r0, a0, r1, a1

    _, _, _, a0, _, a1 = lax.while_loop(cond, body, (jnp.int32(1), live(r0, r1), r0, a0, r1, a1))
    o_ref[0] = jnp.where(first, a0, a1).astype(o_ref.dtype)


def _sb(proj3):
    bsz, seq, _ = proj3.shape
    tq = min(256, seq)
    pw = 2 * SB_HEAD_DIM
    pairs = SB_HEADS // 2
    return pl.pallas_call(
        functools.partial(_sb_kernel, tq=tq),
        grid=(bsz, pairs, seq // tq),
        in_specs=[
            pl.BlockSpec((1, tq, pw), lambda b, p, i: (b, i, COL_SB_Q // pw + p)),
            pl.BlockSpec((1, seq, pw), lambda b, p, i: (b, 0, COL_SB_K // pw + p)),
            pl.BlockSpec((1, seq, pw), lambda b, p, i: (b, 0, COL_SB_V // pw + p)),
        ],
        out_specs=pl.BlockSpec((1, tq, pw), lambda b, p, i: (b, i, p)),
        out_shape=jax.ShapeDtypeStruct((bsz, seq, SB_WIDTH), BF16),
        compiler_params=pltpu.CompilerParams(
            dimension_semantics=("parallel", "parallel", "arbitrary"), vmem_limit_bytes=VMEM_LIMIT),
        name="stick_breaking",
    )(proj3, proj3, proj3)


def _merge_kernel(x_ref, ya_ref, yb_ref, scx_ref, scxh_ref, scc_ref, scch_ref, scb_ref,
                  g0_ref, g1_ref, g2_ref, convw_ref, wbr_ref, wout_ref, nw_ref, o_ref, cbuf,
                  *, tm, kconv):
    t = pl.program_id(1)
    halo = scxh_ref[0].astype(F32) * scch_ref[0].astype(F32)
    cbuf[0:SUBLANES, :] = jnp.where(t == 0, 0.0, halo)
    cbuf[SUBLANES:, :] = scx_ref[0].astype(F32) * scc_ref[0].astype(F32)
    conv = None
    for i in range(kconv):
        term = convw_ref[i:i + 1, :] * cbuf[pl.ds(SUBLANES - (kconv - 1) + i, tm), :]
        conv = term if conv is None else conv + term
    yc = scb_ref[0].astype(F32) * conv
    merged = (_sigmoid(g0_ref[0].astype(F32)) * _dot(ya_ref[0], wbr_ref[0])
              + _sigmoid(g1_ref[0].astype(F32)) * _dot(yb_ref[0], wbr_ref[1])
              + _sigmoid(g2_ref[0].astype(F32)) * _dot(yc.astype(BF16), wbr_ref[2]))
    r = _dot(merged.astype(BF16), wout_ref[...])
    o_ref[0] = x_ref[0] + _rms(r, nw_ref[...])


def _merge(x3, ya, yb, proj3, conv_w, w_br, w_out, nw):
    bsz, seq, d = x3.shape
    kconv = conv_w.shape[0]
    tm = min(512, seq)
    hb = tm // SUBLANES
    sw = SC_WIDTH

    def cur(colblk):
        return lambda b, t: (b, t, colblk)

    def prev(colblk):
        return lambda b, t: (b, jnp.maximum(t * hb - 1, 0), colblk)

    const2 = lambda b, t: (0, 0)
    return pl.pallas_call(
        functools.partial(_merge_kernel, tm=tm, kconv=kconv),
        grid=(bsz, seq // tm),
        in_specs=[
            pl.BlockSpec((1, tm, d), cur(0)),
            pl.BlockSpec((1, tm, GDN_WIDTH), cur(0)),
            pl.BlockSpec((1, tm, SB_WIDTH), cur(0)),
            pl.BlockSpec((1, tm, sw), cur(COL_SC_X // sw)),
            pl.BlockSpec((1, SUBLANES, sw), prev(COL_SC_X // sw)),
            pl.BlockSpec((1, tm, sw), cur(COL_SC_C // sw)),
            pl.BlockSpec((1, SUBLANES, sw), prev(COL_SC_C // sw)),
            pl.BlockSpec((1, tm, sw), cur(COL_SC_B // sw)),
            pl.BlockSpec((1, tm, d), cur(COL_GATES // d)),
            pl.BlockSpec((1, tm, d), cur(COL_GATES // d + 1)),
            pl.BlockSpec((1, tm, d), cur(COL_GATES // d + 2)),
            pl.BlockSpec((kconv, sw), const2),
            pl.BlockSpec((N_BRANCH, sw, d), lambda b, t: (0, 0, 0)),
            pl.BlockSpec((d, d), const2),
            pl.BlockSpec((1, d), const2),
        ],
        out_specs=pl.BlockSpec((1, tm, d), cur(0)),
        out_shape=jax.ShapeDtypeStruct((bsz, seq, d), F32),
        scratch_shapes=[pltpu.VMEM((SUBLANES + tm, sw), F32)],
        compiler_params=pltpu.CompilerParams(
            dimension_semantics=("parallel", "arbitrary"), vmem_limit_bytes=VMEM_LIMIT),
        name="merge",
    )(x3, ya, yb, proj3, proj3, proj3, proj3, proj3, proj3, proj3, proj3, conv_w, w_br, w_out, nw)


def _ffn_kernel(x_ref, nw1_ref, w1_ref, w2_ref, nw2_ref, o_ref, acc_ref):
    j = pl.program_id(1)

    @pl.when(j == 0)
    def _():
        acc_ref[...] = jnp.zeros_like(acc_ref)

    x = x_ref[...]
    h = _rms(x, nw1_ref[...]).astype(BF16)
    f = jnp.maximum(_dot(h, w1_ref[...]), 0.0)
    acc_ref[...] += _dot((f * f).astype(BF16), w2_ref[...])

    @pl.when(j == pl.num_programs(1) - 1)
    def _():
        o_ref[...] = x + _rms(acc_ref[...], nw2_ref[...])


def _ffn(x2, nw1, w1, w2, nw2):
    tokens, d = x2.shape
    dff = w1.shape[1]
    tm = min(1024, tokens)
    tf = min(1024, dff)
    return pl.pallas_call(
        _ffn_kernel,
        grid=(tokens // tm, dff // tf),
        in_specs=[
            pl.BlockSpec((tm, d), lambda i, j: (i, 0)),
            pl.BlockSpec((1, d), lambda i, j: (0, 0)),
            pl.BlockSpec((d, tf), lambda i, j: (0, j)),
            pl.BlockSpec((tf, d), lambda i, j: (j, 0)),
            pl.BlockSpec((1, d), lambda i, j: (0, 0)),
        ],
        out_specs=pl.BlockSpec((tm, d), lambda i, j: (i, 0)),
        out_shape=jax.ShapeDtypeStruct((tokens, d), F32),
        scratch_shapes=[pltpu.VMEM((tm, d), F32)],
        compiler_params=pltpu.CompilerParams(
            dimension_semantics=("parallel", "arbitrary"), vmem_limit_bytes=VMEM_LIMIT),
        name="ffn",
    )(x2, nw1, w1, w2, nw2)


def _split_w_in(w):
    gw, sw = GDN_WIDTH, SB_WIDTH
    o = 0
    gdn_qkv = w[:, o:o + 3 * gw]; o += 3 * gw
    gdn_gate = w[:, o:o + gw]; o += gw
    ab = w[:, o:o + 2 * GDN_HEADS]; o += 2 * GDN_HEADS
    sb_qkv = w[:, o:o + 3 * sw]; o += 3 * sw
    sc_x = w[:, o:o + SC_WIDTH]; o += SC_WIDTH
    sc_b = w[:, o:o + SC_WIDTH]; o += SC_WIDTH
    sc_c = w[:, o:o + SC_WIDTH]; o += SC_WIDTH
    gates = w[:, o:]
    main = jnp.concatenate([gdn_qkv, gdn_gate, sb_qkv, sc_x, sc_c, sc_b, gates], axis=1).astype(BF16)
    ab_pad = jnp.pad(ab, ((0, 0), (0, LANES - 2 * GDN_HEADS))).astype(BF16)
    return main, ab_pad


def _pad_lanes(v, offset=0):
    return jnp.pad(v.astype(F32), (offset, LANES - offset - v.shape[0]))[None, :]


def kernel(x, norm_mix_pre, w_in, conv_qkv_w, gdn_a_log, gdn_dt_bias, gdn_norm_w, conv_sc_w,
           w_branch, w_out, norm_mix_post, norm_ffn_pre, w_ff1, w_ff2, norm_ffn_post):
    bsz, seq, d = x.shape
    depth = w_in.shape[0]
    tokens = bsz * seq
    for l in range(depth):
        w_main, w_ab = _split_w_in(w_in[l])
        proj, ab = _in_proj(x.reshape(tokens, d), norm_mix_pre[l][None, :], w_main, w_ab)
        proj3 = proj.reshape(bsz, seq, -1)
        y_a = _gdn(proj3, ab.reshape(bsz, seq, LANES), conv_qkv_w[l], _pad_lanes(gdn_a_log[l]),
                   _pad_lanes(gdn_dt_bias[l]), gdn_norm_w[l][None, :])
        y_b = _sb(proj3)
        x = _merge(x, y_a, y_b, proj3, conv_sc_w[l], w_branch[l].astype(BF16), w_out[l].astype(BF16),
                   norm_mix_post[l][None, :])
        x = _ffn(x.reshape(tokens, d), norm_ffn_pre[l][None, :], w_ff1[l].astype(BF16),
                 w_ff2[l].astype(BF16), norm_ffn_post[l][None, :]).reshape(bsz, seq, d)
    return x
```

```python
import functools

import jax
import jax.numpy as jnp
from jax import lax
from jax.experimental import pallas as pl
from jax.experimental.pallas import tpu as pltpu

F32 = jnp.float32
BF16 = jnp.bfloat16
EPS = 1e-6

LANES = 128
SUBLANES = 8
VMEM_LIMIT = 56 * 1024 * 1024

GDN_HEADS = 4
GDN_HEAD_DIM = 128
GDN_WIDTH = GDN_HEADS * GDN_HEAD_DIM
GDN_CHUNK = 64
SB_HEADS = 8
SB_HEAD_DIM = 64
SB_WIDTH = SB_HEADS * SB_HEAD_DIM
SC_WIDTH = 512
N_BRANCH = 3

COL_GDN_QKV = 0
COL_GDN_GATE = 3 * GDN_WIDTH
COL_SB_Q = COL_GDN_GATE + GDN_WIDTH
COL_SB_K = COL_SB_Q + SB_WIDTH
COL_SB_V = COL_SB_K + SB_WIDTH
COL_SC_X = COL_SB_V + SB_WIDTH
COL_SC_C = COL_SC_X + SC_WIDTH
COL_SC_B = COL_SC_C + SC_WIDTH
COL_GATES = COL_SC_B + SC_WIDTH


def _sigmoid(x):
    return 1.0 / (1.0 + jnp.exp(-x))


def _softplus(x):
    return jnp.maximum(x, 0.0) + jnp.log(1.0 + jnp.exp(-jnp.abs(x)))


def _rms(x, w):
    return x * lax.rsqrt(jnp.mean(x * x, axis=-1, keepdims=True) + EPS) * w


def _dot(a, b):
    return jnp.dot(a, b, preferred_element_type=F32)


def _dot_nt(a, b):
    return lax.dot_general(a, b, (((1,), (1,)), ((), ())), preferred_element_type=F32)


def _split2(a):
    hi = a.astype(BF16)
    return hi, (a - hi.astype(F32)).astype(BF16)


def _inproj_kernel(x_ref, nw_ref, w_ref, wab_ref, o_ref, oab_ref, h_ref):
    @pl.when(pl.program_id(1) == 0)
    def _():
        hb = _rms(x_ref[...], nw_ref[...]).astype(BF16)
        h_ref[...] = hb
        oab_ref[...] = _dot(hb, wab_ref[...])

    o_ref[...] = _dot(h_ref[...], w_ref[...]).astype(o_ref.dtype)


def _in_proj(x2, nw, w_main, w_ab):
    tokens, d = x2.shape
    n = w_main.shape[1]
    tm = min(1024, tokens)
    tn = min(2048, n)
    return pl.pallas_call(
        _inproj_kernel,
        grid=(tokens // tm, n // tn),
        in_specs=[
            pl.BlockSpec((tm, d), lambda i, j: (i, 0)),
            pl.BlockSpec((1, d), lambda i, j: (0, 0)),
            pl.BlockSpec((d, tn), lambda i, j: (0, j)),
            pl.BlockSpec((d, LANES), lambda i, j: (0, 0)),
        ],
        out_specs=[
            pl.BlockSpec((tm, tn), lambda i, j: (i, j)),
            pl.BlockSpec((tm, LANES), lambda i, j: (i, 0)),
        ],
        out_shape=[
            jax.ShapeDtypeStruct((tokens, n), BF16),
            jax.ShapeDtypeStruct((tokens, LANES), F32),
        ],
        scratch_shapes=[pltpu.VMEM((tm, d), BF16)],
        compiler_params=pltpu.CompilerParams(
            dimension_semantics=("parallel", "arbitrary"), vmem_limit_bytes=VMEM_LIMIT),
        name="in_proj",
    )(x2, nw, w_main, w_ab)


def _bmm(a, b):
    return jnp.einsum("bmk,bkn->bmn", a, b, preferred_element_type=F32)


def _bmm_nt(a, b):
    return jnp.einsum("bmk,bnk->bmn", a, b, preferred_element_type=F32)


def _gdn_kernel(qkv_ref, halo_ref, gate_ref, ab_ref, convw_ref, alog_ref, dtb_ref, nw_ref,
                o_ref, xbuf, state, *, ts, kconv):
    t = pl.program_id(1)
    c_len = GDN_CHUNK
    dk = GDN_HEAD_DIM
    n_ch = ts // c_len

    @pl.when(t == 0)
    def _():
        state[...] = jnp.zeros_like(state)

    xbuf[0:SUBLANES, :] = jnp.where(t == 0, 0.0, halo_ref[0].astype(F32))
    xbuf[SUBLANES:, :] = qkv_ref[0].astype(F32)
    acc = None
    for i in range(kconv):
        term = convw_ref[i:i + 1, :] * xbuf[pl.ds(SUBLANES - (kconv - 1) + i, ts), :]
        acc = term if acc is None else acc + term
    y = acc * _sigmoid(acc)

    ab = ab_ref[0]
    g_all = -jnp.exp(alog_ref[...]) * _softplus(ab + dtb_ref[...])
    beta_all = _sigmoid(ab)

    rt = lax.broadcasted_iota(jnp.int32, (ts, ts), 0)
    ct = lax.broadcasted_iota(jnp.int32, (ts, ts), 1)
    cum_mat = ((rt >= ct) & (rt // c_len == ct // c_len)).astype(BF16)
    g_hi = g_all.astype(BF16)
    g_r32 = g_all - g_hi.astype(F32)
    g_mid = g_r32.astype(BF16)
    g_lo = (g_r32 - g_mid.astype(F32)).astype(BF16)
    gcum = _dot(cum_mat, g_hi) + _dot(cum_mat, g_mid) + _dot(cum_mat, g_lo)

    def per_head(fn):
        return jnp.concatenate([fn(h) for h in range(GDN_HEADS)], axis=0)

    def head_cols(base):
        return per_head(lambda h: y[:, base + h * dk:base + (h + 1) * dk].reshape(n_ch, c_len, dk))

    q = head_cols(0)
    k = head_cols(GDN_WIDTH)
    v = head_cols(2 * GDN_WIDTH)
    q = q * lax.rsqrt(jnp.sum(q * q, axis=-1, keepdims=True) + EPS) * (dk ** -0.5)
    k = k * lax.rsqrt(jnp.sum(k * k, axis=-1, keepdims=True) + EPS)
    gc = per_head(lambda h: gcum[:, h:h + 1].reshape(n_ch, c_len, 1))
    beta = per_head(lambda h: beta_all[:, GDN_HEADS + h:GDN_HEADS + h + 1].reshape(n_ch, c_len, 1))
    gcum_t = [gcum[c * c_len:(c + 1) * c_len, :].T for c in range(n_ch)]
    gr = jnp.stack([gcum_t[c][h:h + 1, :] for h in range(GDN_HEADS) for c in range(n_ch)])
    g_end = gc[:, c_len - 1:c_len, :]

    row = lax.broadcasted_iota(jnp.int32, (c_len, c_len), 0)
    col = lax.broadcasted_iota(jnp.int32, (c_len, c_len), 1)
    tri = (row >= col)[None]
    strict = (row > col)[None]
    eye = (row == col).astype(F32)[None]

    decay = jnp.exp(jnp.where(tri, gc - gr, -1e30))
    egc = jnp.exp(gc)
    k_beta = k * beta
    k16 = k.astype(BF16)
    a_kk = jnp.where(strict, _bmm_nt(k_beta.astype(BF16), k16) * decay, 0.0)
    p = -a_kk
    tinv = eye + p
    p_hi, p_lo = _split2(p)
    for _ in range(5):
        p = _bmm(p_hi, p_hi) + (_bmm(p_hi, p_lo) + _bmm(p_lo, p_hi))
        p_hi, p_lo = _split2(p)
        t_hi, t_lo = _split2(tinv)
        tinv = tinv + (_bmm(t_hi, p_hi) + (_bmm(t_hi, p_lo) + _bmm(t_lo, p_hi)))
    sol = _bmm(tinv.astype(BF16), jnp.concatenate([v * beta, k_beta * egc], axis=-1).astype(BF16))
    u = sol[:, :, :dk]
    w16 = sol[:, :, dk:].astype(BF16)
    a_qk16 = jnp.where(tri, _bmm_nt(q.astype(BF16), k16) * decay, 0.0).astype(BF16)
    q_dec16 = (q * egc).astype(BF16)
    k_dec = k * jnp.exp(g_end - gc)
    s_scale = jnp.exp(g_end)

    nw = nw_ref[...]
    s = [state[h] for h in range(GDN_HEADS)]
    for c in range(n_ch):
        rows = slice(c * c_len, (c + 1) * c_len)
        for h in range(GDN_HEADS):
            i = h * n_ch + c
            s16 = s[h].astype(BF16)
            v_new = u[i] - _dot(w16[i], s16)
            vn16 = v_new.astype(BF16)
            o = _dot(q_dec16[i], s16) + _dot(a_qk16[i], vn16)
            s[h] = s[h] * s_scale[i] + _dot(k_dec[i].T.astype(BF16), vn16)
            o = o * lax.rsqrt(jnp.mean(o * o, axis=-1, keepdims=True) + EPS) * nw
            gate = gate_ref[0, rows, h * dk:(h + 1) * dk].astype(F32)
            o_ref[0, rows, h * dk:(h + 1) * dk] = (o * (gate * _sigmoid(gate))).astype(o_ref.dtype)
    for h in range(GDN_HEADS):
        state[h] = s[h]


def _gdn(proj3, ab3, conv_w, a_log_pad, dt_bias_pad, norm_w):
    bsz, seq, _ = proj3.shape
    kconv = conv_w.shape[0]
    ts = min(256, seq)
    qkv_w = 3 * GDN_WIDTH
    hb = ts // SUBLANES
    return pl.pallas_call(
        functools.partial(_gdn_kernel, ts=ts, kconv=kconv),
        grid=(bsz, seq // ts),
        in_specs=[
            pl.BlockSpec((1, ts, qkv_w), lambda b, t: (b, t, COL_GDN_QKV // qkv_w)),
            pl.BlockSpec((1, SUBLANES, qkv_w),
                         lambda b, t: (b, jnp.maximum(t * hb - 1, 0), COL_GDN_QKV // qkv_w)),
            pl.BlockSpec((1, ts, GDN_WIDTH), lambda b, t: (b, t, COL_GDN_GATE // GDN_WIDTH)),
            pl.BlockSpec((1, ts, LANES), lambda b, t: (b, t, 0)),
            pl.BlockSpec((kconv, qkv_w), lambda b, t: (0, 0)),
            pl.BlockSpec((1, LANES), lambda b, t: (0, 0)),
            pl.BlockSpec((1, LANES), lambda b, t: (0, 0)),
            pl.BlockSpec((1, GDN_HEAD_DIM), lambda b, t: (0, 0)),
        ],
        out_specs=pl.BlockSpec((1, ts, GDN_WIDTH), lambda b, t: (b, t, 0)),
        out_shape=jax.ShapeDtypeStruct((bsz, seq, GDN_WIDTH), BF16),
        scratch_shapes=[
            pltpu.VMEM((SUBLANES + ts, qkv_w), F32),
            pltpu.VMEM((GDN_HEADS, GDN_HEAD_DIM, GDN_HEAD_DIM), F32),
        ],
        compiler_params=pltpu.CompilerParams(
            dimension_semantics=("parallel", "arbitrary"), vmem_limit_bytes=VMEM_LIMIT),
        name="gdn",
    )(proj3, proj3, proj3, ab3, conv_w, a_log_pad, dt_bias_pad, norm_w)


LOG2E = 1.4426950408889634
SB_EXIT = 104.0


def _softplus2(z2):
    neg_abs = lax.bitcast_convert_type(
        lax.bitcast_convert_type(z2, jnp.uint32) | jnp.uint32(0x80000000), F32)
    return jnp.maximum(z2, 0.0) + jnp.log2(1.0 + jnp.exp2(neg_abs))


def _sb_kernel(q_ref, k_ref, v_ref, o_ref, *, tq):
    qb = pl.program_id(2)
    pw = 2 * SB_HEAD_DIM
    q = q_ref[0] * jnp.asarray(SB_HEAD_DIM ** -0.5, BF16)
    first = lax.broadcasted_iota(jnp.int32, (tq, pw), 1) < SB_HEAD_DIM
    zero = jnp.zeros_like(q)
    qs = (jnp.where(first, q, zero), jnp.where(first, zero, q))
    row = lax.broadcasted_iota(jnp.int32, (tq, tq), 0)
    col = lax.broadcasted_iota(jnp.int32, (tq, tq), 1)
    upper = (row > col).astype(BF16)
    upper2 = jnp.concatenate([upper, upper], axis=0)
    causal = col < row
    heads = range(len(qs))

    def sweep(kvs, masks, gates, r_runs, accs):
        zs = [[_dot_nt(qh, k) * LOG2E for qh in qs] for k, _ in kvs]
        sps = [[_softplus2(z) for z in zb] for zb in zs]
        sps = [[sp if m is None else jnp.where(m, sp, 0.0) for sp in sb] for m, sb in zip(masks, sps)]
        afters = [[_dot(jnp.concatenate(_split2(sp), axis=1), upper2) for sp in sb] for sb in sps]
        tots = [[jnp.sum(sp, axis=-1, keepdims=True) for sp in sb] for sb in sps]
        r_runs = list(r_runs)
        accs = list(accs)
        for b, (_, v) in enumerate(kvs):
            for h in heads:
                att = jnp.exp2((zs[b][h] - sps[b][h]) - afters[b][h] - r_runs[h])
                if masks[b] is not None:
                    att = jnp.where(masks[b], att, 0.0)
                pv = _dot(att.astype(BF16), v)
                tot = tots[b][h]
                if gates[b] is not None:
                    pv = jnp.where(gates[b], pv, 0.0)
                    tot = jnp.where(gates[b], tot, 0.0)
                accs[h] = accs[h] + pv
                r_runs[h] = r_runs[h] + tot
        return r_runs, accs

    def kv(kb):
        k0 = pl.multiple_of(kb * tq, tq)
        return k_ref[0, pl.ds(k0, tq), :], v_ref[0, pl.ds(k0, tq), :]

    def done(r_runs):
        return jnp.min(jnp.minimum(r_runs[0], r_runs[1]))

    r_runs, accs = sweep([kv(qb), kv(jnp.maximum(qb - 1, 0))], [causal, None], [None, qb > 0],
                         [jnp.zeros((tq, 1), F32)] * 2, [jnp.zeros((tq, pw), F32)] * 2)

    def cond(c):
        return jnp.logical_and(c[0] <= qb, c[1] < SB_EXIT * LOG2E)

    def body(c):
        i, _, r_runs, accs = c
        r_runs, accs = sweep([kv(qb - i)], [None], [None], r_runs, accs)
        return i + 1, done(r_runs), r_runs, accs

    _, _, _, accs = lax.while_loop(cond, body, (jnp.int32(2), done(r_runs), r_runs, accs))
    o_ref[0] = jnp.where(first, accs[0], accs[1]).astype(o_ref.dtype)


def _sb(proj3):
    bsz, seq, _ = proj3.shape
    tq = min(256, seq)
    pw = 2 * SB_HEAD_DIM
    pairs = SB_HEADS // 2
    return pl.pallas_call(
        functools.partial(_sb_kernel, tq=tq),
        grid=(bsz, pairs, seq // tq),
        in_specs=[
            pl.BlockSpec((1, tq, pw), lambda b, p, i: (b, i, COL_SB_Q // pw + p)),
            pl.BlockSpec((1, seq, pw), lambda b, p, i: (b, 0, COL_SB_K // pw + p)),
            pl.BlockSpec((1, seq, pw), lambda b, p, i: (b, 0, COL_SB_V // pw + p)),
        ],
        out_specs=pl.BlockSpec((1, tq, pw), lambda b, p, i: (b, i, p)),
        out_shape=jax.ShapeDtypeStruct((bsz, seq, SB_WIDTH), BF16),
        compiler_params=pltpu.CompilerParams(
            dimension_semantics=("parallel", "parallel", "arbitrary"), vmem_limit_bytes=VMEM_LIMIT),
        name="stick_breaking",
    )(proj3, proj3, proj3)


def _merge_kernel(x_ref, ya_ref, yb_ref, scx_ref, scxh_ref, scc_ref, scch_ref, scb_ref,
                  g0_ref, g1_ref, g2_ref, convw_ref, wbr_ref, wout_ref, nw_ref, o_ref, cbuf,
                  *, tm, kconv):
    t = pl.program_id(1)
    halo = scxh_ref[0].astype(F32) * scch_ref[0].astype(F32)
    cbuf[0:SUBLANES, :] = jnp.where(t == 0, 0.0, halo)
    cbuf[SUBLANES:, :] = scx_ref[0].astype(F32) * scc_ref[0].astype(F32)
    conv = None
    for i in range(kconv):
        term = convw_ref[i:i + 1, :] * cbuf[pl.ds(SUBLANES - (kconv - 1) + i, tm), :]
        conv = term if conv is None else conv + term
    yc = scb_ref[0].astype(F32) * conv
    merged = (_sigmoid(g0_ref[0].astype(F32)) * _dot(ya_ref[0], wbr_ref[0])
              + _sigmoid(g1_ref[0].astype(F32)) * _dot(yb_ref[0], wbr_ref[1])
              + _sigmoid(g2_ref[0].astype(F32)) * _dot(yc.astype(BF16), wbr_ref[2]))
    r = _dot(merged.astype(BF16), wout_ref[...])
    o_ref[0] = x_ref[0] + _rms(r, nw_ref[...])


def _merge(x3, ya, yb, proj3, conv_w, w_br, w_out, nw):
    bsz, seq, d = x3.shape
    kconv = conv_w.shape[0]
    tm = min(512, seq)
    hb = tm // SUBLANES
    sw = SC_WIDTH

    def cur(colblk):
        return lambda b, t: (b, t, colblk)

    def prev(colblk):
        return lambda b, t: (b, jnp.maximum(t * hb - 1, 0), colblk)

    const2 = lambda b, t: (0, 0)
    return pl.pallas_call(
        functools.partial(_merge_kernel, tm=tm, kconv=kconv),
        grid=(bsz, seq // tm),
        in_specs=[
            pl.BlockSpec((1, tm, d), cur(0)),
            pl.BlockSpec((1, tm, GDN_WIDTH), cur(0)),
            pl.BlockSpec((1, tm, SB_WIDTH), cur(0)),
            pl.BlockSpec((1, tm, sw), cur(COL_SC_X // sw)),
            pl.BlockSpec((1, SUBLANES, sw), prev(COL_SC_X // sw)),
            pl.BlockSpec((1, tm, sw), cur(COL_SC_C // sw)),
            pl.BlockSpec((1, SUBLANES, sw), prev(COL_SC_C // sw)),
            pl.BlockSpec((1, tm, sw), cur(COL_SC_B // sw)),
            pl.BlockSpec((1, tm, d), cur(COL_GATES // d)),
            pl.BlockSpec((1, tm, d), cur(COL_GATES // d + 1)),
            pl.BlockSpec((1, tm, d), cur(COL_GATES // d + 2)),
            pl.BlockSpec((kconv, sw), const2),
            pl.BlockSpec((N_BRANCH, sw, d), lambda b, t: (0, 0, 0)),
            pl.BlockSpec((d, d), const2),
            pl.BlockSpec((1, d), const2),
        ],
        out_specs=pl.BlockSpec((1, tm, d), cur(0)),
        out_shape=jax.ShapeDtypeStruct((bsz, seq, d), F32),
        scratch_shapes=[pltpu.VMEM((SUBLANES + tm, sw), F32)],
        compiler_params=pltpu.CompilerParams(
            dimension_semantics=("parallel", "arbitrary"), vmem_limit_bytes=VMEM_LIMIT),
        name="merge",
    )(x3, ya, yb, proj3, proj3, proj3, proj3, proj3, proj3, proj3, proj3, conv_w, w_br, w_out, nw)


def _ffn_kernel(x_ref, nw1_ref, w1_ref, w2_ref, nw2_ref, o_ref, acc_ref):
    j = pl.program_id(1)

    @pl.when(j == 0)
    def _():
        acc_ref[...] = jnp.zeros_like(acc_ref)

    x = x_ref[...]
    h = _rms(x, nw1_ref[...]).astype(BF16)
    f = jnp.maximum(_dot(h, w1_ref[...]), 0.0)
    acc_ref[...] += _dot((f * f).astype(BF16), w2_ref[...])

    @pl.when(j == pl.num_programs(1) - 1)
    def _():
        o_ref[...] = x + _rms(acc_ref[...], nw2_ref[...])


def _ffn(x2, nw1, w1, w2, nw2):
    tokens, d = x2.shape
    dff = w1.shape[1]
    tm = min(1024, tokens)
    tf = min(1024, dff)
    return pl.pallas_call(
        _ffn_kernel,
        grid=(tokens // tm, dff // tf),
        in_specs=[
            pl.BlockSpec((tm, d), lambda i, j: (i, 0)),
            pl.BlockSpec((1, d), lambda i, j: (0, 0)),
            pl.BlockSpec((d, tf), lambda i, j: (0, j)),
            pl.BlockSpec((tf, d), lambda i, j: (j, 0)),
            pl.BlockSpec((1, d), lambda i, j: (0, 0)),
        ],
        out_specs=pl.BlockSpec((tm, d), lambda i, j: (i, 0)),
        out_shape=jax.ShapeDtypeStruct((tokens, d), F32),
        scratch_shapes=[pltpu.VMEM((tm, d), F32)],
        compiler_params=pltpu.CompilerParams(
            dimension_semantics=("parallel", "arbitrary"), vmem_limit_bytes=VMEM_LIMIT),
        name="ffn",
    )(x2, nw1, w1, w2, nw2)


def _split_w_in(w):
    gw, sw = GDN_WIDTH, SB_WIDTH
    o = 0
    gdn_qkv = w[:, o:o + 3 * gw]; o += 3 * gw
    gdn_gate = w[:, o:o + gw]; o += gw
    ab = w[:, o:o + 2 * GDN_HEADS]; o += 2 * GDN_HEADS
    sb_qkv = w[:, o:o + 3 * sw]; o += 3 * sw
    sc_x = w[:, o:o + SC_WIDTH]; o += SC_WIDTH
    sc_b = w[:, o:o + SC_WIDTH]; o += SC_WIDTH
    sc_c = w[:, o:o + SC_WIDTH]; o += SC_WIDTH
    gates = w[:, o:]
    main = jnp.concatenate([gdn_qkv, gdn_gate, sb_qkv, sc_x, sc_c, sc_b, gates], axis=1).astype(BF16)
    ab_pad = jnp.pad(ab, ((0, 0), (0, LANES - 2 * GDN_HEADS))).astype(BF16)
    return main, ab_pad


def _pad_lanes(v, offset=0):
    return jnp.pad(v.astype(F32), (offset, LANES - offset - v.shape[0]))[None, :]


def kernel(x, norm_mix_pre, w_in, conv_qkv_w, gdn_a_log, gdn_dt_bias, gdn_norm_w, conv_sc_w,
           w_branch, w_out, norm_mix_post, norm_ffn_pre, w_ff1, w_ff2, norm_ffn_post):
    bsz, seq, d = x.shape
    depth = w_in.shape[0]
    tokens = bsz * seq
    for l in range(depth):
        w_main, w_ab = _split_w_in(w_in[l])
        proj, ab = _in_proj(x.reshape(tokens, d), norm_mix_pre[l][None, :], w_main, w_ab)
        proj3 = proj.reshape(bsz, seq, -1)
        y_a = _gdn(proj3, ab.reshape(bsz, seq, LANES), conv_qkv_w[l], _pad_lanes(gdn_a_log[l]),
                   _pad_lanes(gdn_dt_bias[l]), gdn_norm_w[l][None, :])
        y_b = _sb(proj3)
        x = _merge(x, y_a, y_b, proj3, conv_sc_w[l], w_branch[l].astype(BF16), w_out[l].astype(BF16),
                   norm_mix_post[l][None, :])
        x = _ffn(x.reshape(tokens, d), norm_ffn_pre[l][None, :], w_ff1[l].astype(BF16),
                 w_ff2[l].astype(BF16), norm_ffn_post[l][None, :]).reshape(bsz, seq, d)
    return x
```

```python
import functools

import jax
import jax.numpy as jnp
from jax import lax
from jax.experimental import pallas as pl
from jax.experimental.pallas import tpu as pltpu

F32 = jnp.float32
BF16 = jnp.bfloat16
EPS = 1e-6

LANES = 128
SUBLANES = 8
VMEM_LIMIT = 56 * 1024 * 1024

GDN_HEADS = 4
GDN_HEAD_DIM = 128
GDN_WIDTH = GDN_HEADS * GDN_HEAD_DIM
GDN_CHUNK = 64
SB_HEADS = 8
SB_HEAD_DIM = 64
SB_WIDTH = SB_HEADS * SB_HEAD_DIM
SC_WIDTH = 512
N_BRANCH = 3

COL_GDN_QKV = 0
COL_GDN_GATE = 3 * GDN_WIDTH
COL_SB_Q = COL_GDN_GATE + GDN_WIDTH
COL_SB_K = COL_SB_Q + SB_WIDTH
COL_SB_V = COL_SB_K + SB_WIDTH
COL_SC_X = COL_SB_V + SB_WIDTH
COL_SC_C = COL_SC_X + SC_WIDTH
COL_SC_B = COL_SC_C + SC_WIDTH
COL_GATES = COL_SC_B + SC_WIDTH


def _sigmoid(x):
    return 1.0 / (1.0 + jnp.exp(-x))


def _softplus(x):
    return jnp.maximum(x, 0.0) + jnp.log(1.0 + jnp.exp(-jnp.abs(x)))


def _rms(x, w):
    return x * lax.rsqrt(jnp.mean(x * x, axis=-1, keepdims=True) + EPS) * w


def _dot(a, b):
    return jnp.dot(a, b, preferred_element_type=F32)


def _dot_nt(a, b):
    return lax.dot_general(a, b, (((1,), (1,)), ((), ())), preferred_element_type=F32)


def _split2(a):
    hi = a.astype(BF16)
    return hi, (a - hi.astype(F32)).astype(BF16)


def _inproj_kernel(x_ref, nw_ref, w_ref, wab_ref, o_ref, oab_ref, h_ref):
    @pl.when(pl.program_id(1) == 0)
    def _():
        hb = _rms(x_ref[...], nw_ref[...]).astype(BF16)
        h_ref[...] = hb
        oab_ref[...] = _dot(hb, wab_ref[...])

    o_ref[...] = _dot(h_ref[...], w_ref[...]).astype(o_ref.dtype)


def _in_proj(x2, nw, w_main, w_ab):
    tokens, d = x2.shape
    n = w_main.shape[1]
    tm = min(1024, tokens)
    tn = min(2048, n)
    return pl.pallas_call(
        _inproj_kernel,
        grid=(tokens // tm, n // tn),
        in_specs=[
            pl.BlockSpec((tm, d), lambda i, j: (i, 0)),
            pl.BlockSpec((1, d), lambda i, j: (0, 0)),
            pl.BlockSpec((d, tn), lambda i, j: (0, j)),
            pl.BlockSpec((d, LANES), lambda i, j: (0, 0)),
        ],
        out_specs=[
            pl.BlockSpec((tm, tn), lambda i, j: (i, j)),
            pl.BlockSpec((tm, LANES), lambda i, j: (i, 0)),
        ],
        out_shape=[
            jax.ShapeDtypeStruct((tokens, n), BF16),
            jax.ShapeDtypeStruct((tokens, LANES), F32),
        ],
        scratch_shapes=[pltpu.VMEM((tm, d), BF16)],
        compiler_params=pltpu.CompilerParams(
            dimension_semantics=("parallel", "arbitrary"), vmem_limit_bytes=VMEM_LIMIT),
        name="in_proj",
    )(x2, nw, w_main, w_ab)


def _bmm(a, b):
    return jnp.einsum("bmk,bkn->bmn", a, b, preferred_element_type=F32)


def _bmm_nt(a, b):
    return jnp.einsum("bmk,bnk->bmn", a, b, preferred_element_type=F32)


def _gdn_kernel(qkv_ref, halo_ref, gate_ref, ab_ref, convw_ref, alog_ref, dtb_ref, nw_ref, o_ref,
                xbuf, state, u_buf, w_buf, aqk_buf, qdec_buf, kdec_buf, scale_buf, *, ts, kconv):
    t = pl.program_id(1)
    c_len = GDN_CHUNK
    dk = GDN_HEAD_DIM
    n_ch = ts // c_len

    @pl.when(t == 0)
    def _():
        state[...] = jnp.zeros_like(state)
        u_buf[...] = jnp.zeros_like(u_buf)
        w_buf[...] = jnp.zeros_like(w_buf)
        aqk_buf[...] = jnp.zeros_like(aqk_buf)
        qdec_buf[...] = jnp.zeros_like(qdec_buf)
        kdec_buf[...] = jnp.zeros_like(kdec_buf)
        scale_buf[...] = jnp.zeros_like(scale_buf)

    nw = nw_ref[...]
    s = [state[h] for h in range(GDN_HEADS)]
    pend = {}

    def scan_first(c, h):
        i = h * n_ch + c
        s16 = s[h].astype(BF16)
        pend[h] = (u_buf[i] - _dot(w_buf[i], s16), _dot(qdec_buf[i], s16))

    def scan_second(c, h):
        rows = slice(c * c_len, (c + 1) * c_len)
        i = h * n_ch + c
        v_new, o_state = pend[h]
        vn16 = v_new.astype(BF16)
        o = o_state + _dot(aqk_buf[i], vn16)
        s[h] = s[h] * scale_buf[i, 0:1, :] + _dot(kdec_buf[i].T.astype(BF16), vn16)
        o = o * lax.rsqrt(jnp.mean(o * o, axis=-1, keepdims=True) + EPS) * nw
        gate = gate_ref[0, rows, h * dk:(h + 1) * dk].astype(F32)
        o_ref[0, rows, h * dk:(h + 1) * dk] = (o * (gate * _sigmoid(gate))).astype(o_ref.dtype)

    half_steps = [functools.partial(f, c, h) for c in range(n_ch) for f in (scan_first, scan_second)
                  for h in range(GDN_HEADS)]

    def scan_advance(n=1):
        for _ in range(n):
            if half_steps:
                half_steps.pop(0)()

    xbuf[0:SUBLANES, :] = jnp.where(t == 0, 0.0, halo_ref[0].astype(F32))
    xbuf[SUBLANES:, :] = qkv_ref[0].astype(F32)
    acc = None
    for i in range(kconv):
        term = convw_ref[i:i + 1, :] * xbuf[pl.ds(SUBLANES - (kconv - 1) + i, ts), :]
        acc = term if acc is None else acc + term
    y = acc * _sigmoid(acc)

    ab = ab_ref[0]
    g_all = -jnp.exp(alog_ref[...]) * _softplus(ab + dtb_ref[...])
    beta_all = _sigmoid(ab)

    rt = lax.broadcasted_iota(jnp.int32, (ts, ts), 0)
    ct = lax.broadcasted_iota(jnp.int32, (ts, ts), 1)
    cum_mat = ((rt >= ct) & (rt // c_len == ct // c_len)).astype(BF16)
    g_hi = g_all.astype(BF16)
    g_r32 = g_all - g_hi.astype(F32)
    g_mid = g_r32.astype(BF16)
    g_lo = (g_r32 - g_mid.astype(F32)).astype(BF16)
    gcum = _dot(cum_mat, g_hi) + _dot(cum_mat, g_mid) + _dot(cum_mat, g_lo)

    def per_head(fn):
        return jnp.concatenate([fn(h) for h in range(GDN_HEADS)], axis=0)

    def head_cols(base):
        return per_head(lambda h: y[:, base + h * dk:base + (h + 1) * dk].reshape(n_ch, c_len, dk))

    q = head_cols(0)
    k = head_cols(GDN_WIDTH)
    v = head_cols(2 * GDN_WIDTH)
    q = q * lax.rsqrt(jnp.sum(q * q, axis=-1, keepdims=True) + EPS) * (dk ** -0.5)
    k = k * lax.rsqrt(jnp.sum(k * k, axis=-1, keepdims=True) + EPS)
    gc = per_head(lambda h: gcum[:, h:h + 1].reshape(n_ch, c_len, 1))
    beta = per_head(lambda h: beta_all[:, GDN_HEADS + h:GDN_HEADS + h + 1].reshape(n_ch, c_len, 1))
    gcum_t = [gcum[c * c_len:(c + 1) * c_len, :].T for c in range(n_ch)]
    gr = jnp.stack([gcum_t[c][h:h + 1, :] for h in range(GDN_HEADS) for c in range(n_ch)])
    g_end = gc[:, c_len - 1:c_len, :]

    row = lax.broadcasted_iota(jnp.int32, (c_len, c_len), 0)
    col = lax.broadcasted_iota(jnp.int32, (c_len, c_len), 1)
    tri = (row >= col)[None]
    strict = (row > col)[None]
    eye = (row == col).astype(F32)[None]

    decay = jnp.exp(jnp.where(tri, gc - gr, -1e30))
    egc = jnp.exp(gc)
    k_beta = k * beta
    k16 = k.astype(BF16)
    def bmm_s(a, b):
        scan_advance()
        return _bmm(a, b)

    scan_advance()
    a_kk = jnp.where(strict, _bmm_nt(k_beta.astype(BF16), k16) * decay, 0.0)
    p = -a_kk
    tinv = eye + p
    p_hi, p_lo = _split2(p)
    for _ in range(5):
        p = bmm_s(p_hi, p_hi) + (bmm_s(p_hi, p_lo) + bmm_s(p_lo, p_hi))
        p_hi, p_lo = _split2(p)
        t_hi, t_lo = _split2(tinv)
        tinv = tinv + (bmm_s(t_hi, p_hi) + (bmm_s(t_hi, p_lo) + bmm_s(t_lo, p_hi)))
    sol = bmm_s(tinv.astype(BF16), jnp.concatenate([v * beta, k_beta * egc], axis=-1).astype(BF16))
    a_qk = jnp.where(tri, _bmm_nt(q.astype(BF16), k16) * decay, 0.0)
    scan_advance(len(half_steps))
    for h in range(GDN_HEADS):
        state[h] = s[h]

    u_buf[...] = sol[:, :, :dk]
    w_buf[...] = sol[:, :, dk:].astype(BF16)
    aqk_buf[...] = a_qk.astype(BF16)
    qdec_buf[...] = (q * egc).astype(BF16)
    kdec_buf[...] = k * jnp.exp(g_end - gc)
    scale_buf[...] = jnp.broadcast_to(jnp.exp(g_end), scale_buf.shape)


def _gdn(proj3, ab3, conv_w, a_log_pad, dt_bias_pad, norm_w):
    bsz, seq, _ = proj3.shape
    kconv = conv_w.shape[0]
    ts = min(256, seq)
    qkv_w = 3 * GDN_WIDTH
    hb = ts // SUBLANES
    n_t = seq // ts
    nb = GDN_HEADS * (ts // GDN_CHUNK)
    dk = GDN_HEAD_DIM

    def ahead(colblk):
        return lambda b, t: (b, jnp.minimum(t, n_t - 1), colblk)

    def behind(colblk):
        return lambda b, t: (b, jnp.maximum(t - 1, 0), colblk)

    return pl.pallas_call(
        functools.partial(_gdn_kernel, ts=ts, kconv=kconv),
        grid=(bsz, n_t + 1),
        in_specs=[
            pl.BlockSpec((1, ts, qkv_w), ahead(COL_GDN_QKV // qkv_w)),
            pl.BlockSpec((1, SUBLANES, qkv_w),
                         lambda b, t: (b, jnp.maximum(jnp.minimum(t, n_t - 1) * hb - 1, 0), COL_GDN_QKV // qkv_w)),
            pl.BlockSpec((1, ts, GDN_WIDTH), behind(COL_GDN_GATE // GDN_WIDTH)),
            pl.BlockSpec((1, ts, LANES), ahead(0)),
            pl.BlockSpec((kconv, qkv_w), lambda b, t: (0, 0)),
            pl.BlockSpec((1, LANES), lambda b, t: (0, 0)),
            pl.BlockSpec((1, LANES), lambda b, t: (0, 0)),
            pl.BlockSpec((1, GDN_HEAD_DIM), lambda b, t: (0, 0)),
        ],
        out_specs=pl.BlockSpec((1, ts, GDN_WIDTH), behind(0)),
        out_shape=jax.ShapeDtypeStruct((bsz, seq, GDN_WIDTH), BF16),
        scratch_shapes=[
            pltpu.VMEM((SUBLANES + ts, qkv_w), F32),
            pltpu.VMEM((GDN_HEADS, dk, dk), F32),
            pltpu.VMEM((nb, GDN_CHUNK, dk), F32),
            pltpu.VMEM((nb, GDN_CHUNK, dk), BF16),
            pltpu.VMEM((nb, GDN_CHUNK, GDN_CHUNK), BF16),
            pltpu.VMEM((nb, GDN_CHUNK, dk), BF16),
            pltpu.VMEM((nb, GDN_CHUNK, dk), F32),
            pltpu.VMEM((nb, SUBLANES, dk), F32),
        ],
        compiler_params=pltpu.CompilerParams(
            dimension_semantics=("parallel", "arbitrary"), vmem_limit_bytes=VMEM_LIMIT),
        name="gdn",
    )(proj3, proj3, proj3, ab3, conv_w, a_log_pad, dt_bias_pad, norm_w)


LOG2E = 1.4426950408889634
SB_EXIT = 104.0


def _softplus2(z2):
    neg_abs = lax.bitcast_convert_type(
        lax.bitcast_convert_type(z2, jnp.uint32) | jnp.uint32(0x80000000), F32)
    return jnp.maximum(z2, 0.0) + jnp.log2(1.0 + jnp.exp2(neg_abs))


def _sb_kernel(q_ref, k_ref, v_ref, o_ref, *, tq):
    qb = pl.program_id(2)
    pw = 2 * SB_HEAD_DIM
    q = q_ref[0] * jnp.asarray(SB_HEAD_DIM ** -0.5, BF16)
    first = lax.broadcasted_iota(jnp.int32, (tq, pw), 1) < SB_HEAD_DIM
    zero = jnp.zeros_like(q)
    qs = (jnp.where(first, q, zero), jnp.where(first, zero, q))
    row = lax.broadcasted_iota(jnp.int32, (tq, tq), 0)
    col = lax.broadcasted_iota(jnp.int32, (tq, tq), 1)
    upper = (row > col).astype(BF16)
    upper2 = jnp.concatenate([upper, upper], axis=0)
    causal = col < row
    heads = range(len(qs))

    def sweep(kvs, masks, gates, r_runs, accs):
        zs = [[_dot_nt(qh, k) * LOG2E for qh in qs] for k, _ in kvs]
        sps = [[_softplus2(z) for z in zb] for zb in zs]
        sps = [[sp if m is None else jnp.where(m, sp, 0.0) for sp in sb] for m, sb in zip(masks, sps)]
        afters = [[_dot(jnp.concatenate(_split2(sp), axis=1), upper2) for sp in sb] for sb in sps]
        tots = [[jnp.sum(sp, axis=-1, keepdims=True) for sp in sb] for sb in sps]
        r_runs = list(r_runs)
        accs = list(accs)
        for b, (_, v) in enumerate(kvs):
            for h in heads:
                att = jnp.exp2((zs[b][h] - sps[b][h]) - afters[b][h] - r_runs[h])
                if masks[b] is not None:
                    att = jnp.where(masks[b], att, 0.0)
                pv = _dot(att.astype(BF16), v)
                tot = tots[b][h]
                if gates[b] is not None:
                    pv = jnp.where(gates[b], pv, 0.0)
                    tot = jnp.where(gates[b], tot, 0.0)
                accs[h] = accs[h] + pv
                r_runs[h] = r_runs[h] + tot
        return r_runs, accs

    def kv(kb):
        k0 = pl.multiple_of(kb * tq, tq)
        return k_ref[0, pl.ds(k0, tq), :], v_ref[0, pl.ds(k0, tq), :]

    def done(r_runs):
        return jnp.min(jnp.minimum(r_runs[0], r_runs[1]))

    r_runs, accs = sweep([kv(qb), kv(jnp.maximum(qb - 1, 0))], [causal, None], [None, qb > 0],
                         [jnp.zeros((tq, 1), F32)] * 2, [jnp.zeros((tq, pw), F32)] * 2)

    def cond(c):
        return jnp.logical_and(c[0] <= qb, c[1] < SB_EXIT * LOG2E)

    def body(c):
        i, _, r_runs, accs = c
        r_runs, accs = sweep([kv(qb - i)], [None], [None], r_runs, accs)
        return i + 1, done(r_runs), r_runs, accs

    _, _, _, accs = lax.while_loop(cond, body, (jnp.int32(2), done(r_runs), r_runs, accs))
    o_ref[0] = jnp.where(first, accs[0], accs[1]).astype(o_ref.dtype)


def _sb(proj3):
    bsz, seq, _ = proj3.shape
    tq = min(256, seq)
    pw = 2 * SB_HEAD_DIM
    pairs = SB_HEADS // 2
    return pl.pallas_call(
        functools.partial(_sb_kernel, tq=tq),
        grid=(bsz, pairs, seq // tq),
        in_specs=[
            pl.BlockSpec((1, tq, pw), lambda b, p, i: (b, i, COL_SB_Q // pw + p)),
            pl.BlockSpec((1, seq, pw), lambda b, p, i: (b, 0, COL_SB_K // pw + p)),
            pl.BlockSpec((1, seq, pw), lambda b, p, i: (b, 0, COL_SB_V // pw + p)),
        ],
        out_specs=pl.BlockSpec((1, tq, pw), lambda b, p, i: (b, i, p)),
        out_shape=jax.ShapeDtypeStruct((bsz, seq, SB_WIDTH), BF16),
        compiler_params=pltpu.CompilerParams(
            dimension_semantics=("parallel", "parallel", "arbitrary"), vmem_limit_bytes=VMEM_LIMIT),
        name="stick_breaking",
    )(proj3, proj3, proj3)


def _merge_kernel(x_ref, ya_ref, yb_ref, scx_ref, scxh_ref, scc_ref, scch_ref, scb_ref,
                  g0_ref, g1_ref, g2_ref, convw_ref, wbr_ref, wout_ref, nw_ref, o_ref, cbuf,
                  *, tm, kconv):
    t = pl.program_id(1)
    halo = scxh_ref[0].astype(F32) * scch_ref[0].astype(F32)
    cbuf[0:SUBLANES, :] = jnp.where(t == 0, 0.0, halo)
    cbuf[SUBLANES:, :] = scx_ref[0].astype(F32) * scc_ref[0].astype(F32)
    conv = None
    for i in range(kconv):
        term = convw_ref[i:i + 1, :] * cbuf[pl.ds(SUBLANES - (kconv - 1) + i, tm), :]
        conv = term if conv is None else conv + term
    yc = scb_ref[0].astype(F32) * conv
    merged = (_sigmoid(g0_ref[0].astype(F32)) * _dot(ya_ref[0], wbr_ref[0])
              + _sigmoid(g1_ref[0].astype(F32)) * _dot(yb_ref[0], wbr_ref[1])
              + _sigmoid(g2_ref[0].astype(F32)) * _dot(yc.astype(BF16), wbr_ref[2]))
    r = _dot(merged.astype(BF16), wout_ref[...])
    o_ref[0] = x_ref[0] + _rms(r, nw_ref[...])


def _merge(x3, ya, yb, proj3, conv_w, w_br, w_out, nw):
    bsz, seq, d = x3.shape
    kconv = conv_w.shape[0]
    tm = min(512, seq)
    hb = tm // SUBLANES
    sw = SC_WIDTH

    def cur(colblk):
        return lambda b, t: (b, t, colblk)

    def prev(colblk):
        return lambda b, t: (b, jnp.maximum(t * hb - 1, 0), colblk)

    const2 = lambda b, t: (0, 0)
    return pl.pallas_call(
        functools.partial(_merge_kernel, tm=tm, kconv=kconv),
        grid=(bsz, seq // tm),
        in_specs=[
            pl.BlockSpec((1, tm, d), cur(0)),
            pl.BlockSpec((1, tm, GDN_WIDTH), cur(0)),
            pl.BlockSpec((1, tm, SB_WIDTH), cur(0)),
            pl.BlockSpec((1, tm, sw), cur(COL_SC_X // sw)),
            pl.BlockSpec((1, SUBLANES, sw), prev(COL_SC_X // sw)),
            pl.BlockSpec((1, tm, sw), cur(COL_SC_C // sw)),
            pl.BlockSpec((1, SUBLANES, sw), prev(COL_SC_C // sw)),
            pl.BlockSpec((1, tm, sw), cur(COL_SC_B // sw)),
            pl.BlockSpec((1, tm, d), cur(COL_GATES // d)),
            pl.BlockSpec((1, tm, d), cur(COL_GATES // d + 1)),
            pl.BlockSpec((1, tm, d), cur(COL_GATES // d + 2)),
            pl.BlockSpec((kconv, sw), const2),
            pl.BlockSpec((N_BRANCH, sw, d), lambda b, t: (0, 0, 0)),
            pl.BlockSpec((d, d), const2),
            pl.BlockSpec((1, d), const2),
        ],
        out_specs=pl.BlockSpec((1, tm, d), cur(0)),
        out_shape=jax.ShapeDtypeStruct((bsz, seq, d), F32),
        scratch_shapes=[pltpu.VMEM((SUBLANES + tm, sw), F32)],
        compiler_params=pltpu.CompilerParams(
            dimension_semantics=("parallel", "arbitrary"), vmem_limit_bytes=VMEM_LIMIT),
        name="merge",
    )(x3, ya, yb, proj3, proj3, proj3, proj3, proj3, proj3, proj3, proj3, conv_w, w_br, w_out, nw)


def _ffn_kernel(x_ref, nw1_ref, w1_ref, w2_ref, nw2_ref, o_ref, acc_ref):
    j = pl.program_id(1)

    @pl.when(j == 0)
    def _():
        acc_ref[...] = jnp.zeros_like(acc_ref)

    x = x_ref[...]
    h = _rms(x, nw1_ref[...]).astype(BF16)
    f = jnp.maximum(_dot(h, w1_ref[...]), 0.0)
    acc_ref[...] += _dot((f * f).astype(BF16), w2_ref[...])

    @pl.when(j == pl.num_programs(1) - 1)
    def _():
        o_ref[...] = x + _rms(acc_ref[...], nw2_ref[...])


def _ffn(x2, nw1, w1, w2, nw2):
    tokens, d = x2.shape
    dff = w1.shape[1]
    tm = min(1024, tokens)
    tf = min(1024, dff)
    return pl.pallas_call(
        _ffn_kernel,
        grid=(tokens // tm, dff // tf),
        in_specs=[
            pl.BlockSpec((tm, d), lambda i, j: (i, 0)),
            pl.BlockSpec((1, d), lambda i, j: (0, 0)),
            pl.BlockSpec((d, tf), lambda i, j: (0, j)),
            pl.BlockSpec((tf, d), lambda i, j: (j, 0)),
            pl.BlockSpec((1, d), lambda i, j: (0, 0)),
        ],
        out_specs=pl.BlockSpec((tm, d), lambda i, j: (i, 0)),
        out_shape=jax.ShapeDtypeStruct((tokens, d), F32),
        scratch_shapes=[pltpu.VMEM((tm, d), F32)],
        compiler_params=pltpu.CompilerParams(
            dimension_semantics=("parallel", "arbitrary"), vmem_limit_bytes=VMEM_LIMIT),
        name="ffn",
    )(x2, nw1, w1, w2, nw2)


def _split_w_in(w):
    gw, sw = GDN_WIDTH, SB_WIDTH
    o = 0
    gdn_qkv = w[:, o:o + 3 * gw]; o += 3 * gw
    gdn_gate = w[:, o:o + gw]; o += gw
    ab = w[:, o:o + 2 * GDN_HEADS]; o += 2 * GDN_HEADS
    sb_qkv = w[:, o:o + 3 * sw]; o += 3 * sw
    sc_x = w[:, o:o + SC_WIDTH]; o += SC_WIDTH
    sc_b = w[:, o:o + SC_WIDTH]; o += SC_WIDTH
    sc_c = w[:, o:o + SC_WIDTH]; o += SC_WIDTH
    gates = w[:, o:]
    main = jnp.concatenate([gdn_qkv, gdn_gate, sb_qkv, sc_x, sc_c, sc_b, gates], axis=1).astype(BF16)
    ab_pad = jnp.pad(ab, ((0, 0), (0, LANES - 2 * GDN_HEADS))).astype(BF16)
    return main, ab_pad


def _pad_lanes(v, offset=0):
    return jnp.pad(v.astype(F32), (offset, LANES - offset - v.shape[0]))[None, :]


def kernel(x, norm_mix_pre, w_in, conv_qkv_w, gdn_a_log, gdn_dt_bias, gdn_norm_w, conv_sc_w,
           w_branch, w_out, norm_mix_post, norm_ffn_pre, w_ff1, w_ff2, norm_ffn_post):
    bsz, seq, d = x.shape
    depth = w_in.shape[0]
    tokens = bsz * seq
    for l in range(depth):
        w_main, w_ab = _split_w_in(w_in[l])
        proj, ab = _in_proj(x.reshape(tokens, d), norm_mix_pre[l][None, :], w_main, w_ab)
        proj3 = proj.reshape(bsz, seq, -1)
        y_a = _gdn(proj3, ab.reshape(bsz, seq, LANES), conv_qkv_w[l], _pad_lanes(gdn_a_log[l]),
                   _pad_lanes(gdn_dt_bias[l]), gdn_norm_w[l][None, :])
        y_b = _sb(proj3)
        x = _merge(x, y_a, y_b, proj3, conv_sc_w[l], w_branch[l].astype(BF16), w_out[l].astype(BF16),
                   norm_mix_post[l][None, :])
        x = _ffn(x.reshape(tokens, d), norm_ffn_pre[l][None, :], w_ff1[l].astype(BF16),
                 w_ff2[l].astype(BF16), norm_ffn_post[l][None, :]).reshape(bsz, seq, d)
    return x
```

```python
import functools

import jax
import jax.numpy as jnp
from jax import lax
from jax.experimental import pallas as pl
from jax.experimental.pallas import tpu as pltpu

F32 = jnp.float32
BF16 = jnp.bfloat16
EPS = 1e-6

LANES = 128
SUBLANES = 8
VMEM_LIMIT = 56 * 1024 * 1024

GDN_HEADS = 4
GDN_HEAD_DIM = 128
GDN_WIDTH = GDN_HEADS * GDN_HEAD_DIM
GDN_CHUNK = 64
SB_HEADS = 8
SB_HEAD_DIM = 64
SB_WIDTH = SB_HEADS * SB_HEAD_DIM
SC_WIDTH = 512
N_BRANCH = 3

COL_GDN_QKV = 0
COL_GDN_GATE = 3 * GDN_WIDTH
COL_SB_Q = COL_GDN_GATE + GDN_WIDTH
COL_SB_K = COL_SB_Q + SB_WIDTH
COL_SB_V = COL_SB_K + SB_WIDTH
COL_SC_X = COL_SB_V + SB_WIDTH
COL_SC_C = COL_SC_X + SC_WIDTH
COL_SC_B = COL_SC_C + SC_WIDTH
COL_GATES = COL_SC_B + SC_WIDTH


def _sigmoid(x):
    return 0.5 * jnp.tanh(0.5 * x) + 0.5


def _softplus(x):
    return jnp.maximum(x, 0.0) + jnp.log(1.0 + jnp.exp(-jnp.abs(x)))


def _rms(x, w):
    return x * lax.rsqrt(jnp.mean(x * x, axis=-1, keepdims=True) + EPS) * w


def _dot(a, b):
    return jnp.dot(a, b, preferred_element_type=F32)


def _dot_nt(a, b):
    return lax.dot_general(a, b, (((1,), (1,)), ((), ())), preferred_element_type=F32)


def _split2(a):
    hi = a.astype(BF16)
    return hi, (a - hi.astype(F32)).astype(BF16)


def _inproj_kernel(x_ref, nw_ref, w_ref, wab_ref, o_ref, oab_ref, *, tn):
    hb = _rms(x_ref[...], nw_ref[...]).astype(BF16)
    oab_ref[...] = _dot(hb, wab_ref[...])
    for c in range(w_ref.shape[1] // tn):
        o_ref[:, c * tn:(c + 1) * tn] = _dot(hb, w_ref[:, c * tn:(c + 1) * tn]).astype(o_ref.dtype)


def _in_proj(x2, nw, w_main, w_ab):
    tokens, d = x2.shape
    n = w_main.shape[1]
    tm = min(512, tokens)
    tn = min(2048, n)
    resident = pl.Buffered(1)
    return pl.pallas_call(
        functools.partial(_inproj_kernel, tn=tn),
        grid=(tokens // tm,),
        in_specs=[
            pl.BlockSpec((tm, d), lambda i: (i, 0)),
            pl.BlockSpec((1, d), lambda i: (0, 0)),
            pl.BlockSpec((d, n), lambda i: (0, 0), pipeline_mode=resident),
            pl.BlockSpec((d, LANES), lambda i: (0, 0)),
        ],
        out_specs=[
            pl.BlockSpec((tm, n), lambda i: (i, 0)),
            pl.BlockSpec((tm, LANES), lambda i: (i, 0)),
        ],
        out_shape=[
            jax.ShapeDtypeStruct((tokens, n), BF16),
            jax.ShapeDtypeStruct((tokens, LANES), F32),
        ],
        compiler_params=pltpu.CompilerParams(
            dimension_semantics=("parallel",), vmem_limit_bytes=VMEM_LIMIT),
        name="in_proj",
    )(x2, nw, w_main, w_ab)


def _bmm(a, b):
    return jnp.einsum("bmk,bkn->bmn", a, b, preferred_element_type=F32)


def _bmm_nt(a, b):
    return jnp.einsum("bmk,bnk->bmn", a, b, preferred_element_type=F32)


def _gdn_kernel(qkv_ref, halo_ref, gate_ref, ab_ref, convw_ref, alog_ref, dtb_ref, nw_ref, o_ref,
                xbuf, state, u_buf, w_buf, aqk_buf, qdec_buf, kdec_buf, scale_buf, *, ts, kconv):
    t = pl.program_id(1)
    c_len = GDN_CHUNK
    dk = GDN_HEAD_DIM
    n_ch = ts // c_len

    @pl.when(t == 0)
    def _():
        state[...] = jnp.zeros_like(state)
        u_buf[...] = jnp.zeros_like(u_buf)
        w_buf[...] = jnp.zeros_like(w_buf)
        aqk_buf[...] = jnp.zeros_like(aqk_buf)
        qdec_buf[...] = jnp.zeros_like(qdec_buf)
        kdec_buf[...] = jnp.zeros_like(kdec_buf)
        scale_buf[...] = jnp.zeros_like(scale_buf)

    nw = nw_ref[...]
    s = [state[h] for h in range(GDN_HEADS)]
    pend = {}

    def scan_first(c, h):
        i = h * n_ch + c
        s16 = s[h].astype(BF16)
        pend[h] = (u_buf[i] - _dot(w_buf[i], s16), _dot(qdec_buf[i], s16))

    def scan_second(c, h):
        rows = slice(c * c_len, (c + 1) * c_len)
        i = h * n_ch + c
        v_new, o_state = pend[h]
        vn16 = v_new.astype(BF16)
        o = o_state + _dot(aqk_buf[i], vn16)
        s[h] = s[h] * scale_buf[i, 0:1, :] + _dot(kdec_buf[i].T.astype(BF16), vn16)
        o = o * lax.rsqrt(jnp.mean(o * o, axis=-1, keepdims=True) + EPS) * nw
        gate = gate_ref[0, rows, h * dk:(h + 1) * dk].astype(F32)
        o_ref[0, rows, h * dk:(h + 1) * dk] = (o * (gate * _sigmoid(gate))).astype(o_ref.dtype)

    half_steps = [functools.partial(f, c, h) for c in range(n_ch) for f in (scan_first, scan_second)
                  for h in range(GDN_HEADS)]

    def scan_advance(n=1):
        for _ in range(n):
            if half_steps:
                half_steps.pop(0)()

    xbuf[0:SUBLANES, :] = jnp.where(t == 0, 0.0, halo_ref[0].astype(F32))
    xbuf[SUBLANES:, :] = qkv_ref[0].astype(F32)
    acc = None
    for i in range(kconv):
        term = convw_ref[i:i + 1, :] * xbuf[pl.ds(SUBLANES - (kconv - 1) + i, ts), :]
        acc = term if acc is None else acc + term
    y = acc * _sigmoid(acc)

    ab = ab_ref[0]
    g_all = -jnp.exp(alog_ref[...]) * _softplus(ab + dtb_ref[...])
    beta_all = _sigmoid(ab)

    rt = lax.broadcasted_iota(jnp.int32, (ts, ts), 0)
    ct = lax.broadcasted_iota(jnp.int32, (ts, ts), 1)
    cum_mat = ((rt >= ct) & (rt // c_len == ct // c_len)).astype(BF16)
    g_hi = g_all.astype(BF16)
    g_r32 = g_all - g_hi.astype(F32)
    g_mid = g_r32.astype(BF16)
    g_lo = (g_r32 - g_mid.astype(F32)).astype(BF16)
    gcum = _dot(cum_mat, g_hi) + _dot(cum_mat, g_mid) + _dot(cum_mat, g_lo)

    def per_head(fn):
        return jnp.concatenate([fn(h) for h in range(GDN_HEADS)], axis=0)

    def head_cols(base):
        return per_head(lambda h: y[:, base + h * dk:base + (h + 1) * dk].reshape(n_ch, c_len, dk))

    q = head_cols(0)
    k = head_cols(GDN_WIDTH)
    v = head_cols(2 * GDN_WIDTH)
    q = q * lax.rsqrt(jnp.sum(q * q, axis=-1, keepdims=True) + EPS) * (dk ** -0.5)
    k = k * lax.rsqrt(jnp.sum(k * k, axis=-1, keepdims=True) + EPS)
    gc = per_head(lambda h: gcum[:, h:h + 1].reshape(n_ch, c_len, 1))
    beta = per_head(lambda h: beta_all[:, GDN_HEADS + h:GDN_HEADS + h + 1].reshape(n_ch, c_len, 1))
    gcum_t = [gcum[c * c_len:(c + 1) * c_len, :].T for c in range(n_ch)]
    gr = jnp.stack([gcum_t[c][h:h + 1, :] for h in range(GDN_HEADS) for c in range(n_ch)])
    g_end = gc[:, c_len - 1:c_len, :]

    row = lax.broadcasted_iota(jnp.int32, (c_len, c_len), 0)
    col = lax.broadcasted_iota(jnp.int32, (c_len, c_len), 1)
    tri = (row >= col)[None]
    strict = (row > col)[None]
    eye = (row == col).astype(F32)[None]

    decay = jnp.exp(jnp.where(tri, gc - gr, -1e30))
    egc = jnp.exp(gc)
    k_beta = k * beta
    k16 = k.astype(BF16)
    def bmm_s(a, b):
        scan_advance()
        return _bmm(a, b)

    scan_advance()
    a_kk = jnp.where(strict, _bmm_nt(k_beta.astype(BF16), k16) * decay, 0.0)
    p = -a_kk
    tinv = eye + p
    p_hi, p_lo = _split2(p)
    for _ in range(5):
        p = bmm_s(p_hi, p_hi) + (bmm_s(p_hi, p_lo) + bmm_s(p_lo, p_hi))
        p_hi, p_lo = _split2(p)
        t_hi, t_lo = _split2(tinv)
        tinv = tinv + (bmm_s(t_hi, p_hi) + (bmm_s(t_hi, p_lo) + bmm_s(t_lo, p_hi)))
    sol = bmm_s(tinv.astype(BF16), jnp.concatenate([v * beta, k_beta * egc], axis=-1).astype(BF16))
    a_qk = jnp.where(tri, _bmm_nt(q.astype(BF16), k16) * decay, 0.0)
    scan_advance(len(half_steps))
    for h in range(GDN_HEADS):
        state[h] = s[h]

    u_buf[...] = sol[:, :, :dk]
    w_buf[...] = sol[:, :, dk:].astype(BF16)
    aqk_buf[...] = a_qk.astype(BF16)
    qdec_buf[...] = (q * egc).astype(BF16)
    kdec_buf[...] = k * jnp.exp(g_end - gc)
    scale_buf[...] = jnp.broadcast_to(jnp.exp(g_end), scale_buf.shape)


def _gdn(proj3, ab3, conv_w, a_log_pad, dt_bias_pad, norm_w):
    bsz, seq, _ = proj3.shape
    kconv = conv_w.shape[0]
    ts = min(256, seq)
    qkv_w = 3 * GDN_WIDTH
    hb = ts // SUBLANES
    n_t = seq // ts
    nb = GDN_HEADS * (ts // GDN_CHUNK)
    dk = GDN_HEAD_DIM

    def ahead(colblk):
        return lambda b, t: (b, jnp.minimum(t, n_t - 1), colblk)

    def behind(colblk):
        return lambda b, t: (b, jnp.maximum(t - 1, 0), colblk)

    return pl.pallas_call(
        functools.partial(_gdn_kernel, ts=ts, kconv=kconv),
        grid=(bsz, n_t + 1),
        in_specs=[
            pl.BlockSpec((1, ts, qkv_w), ahead(COL_GDN_QKV // qkv_w)),
            pl.BlockSpec((1, SUBLANES, qkv_w),
                         lambda b, t: (b, jnp.maximum(jnp.minimum(t, n_t - 1) * hb - 1, 0), COL_GDN_QKV // qkv_w)),
            pl.BlockSpec((1, ts, GDN_WIDTH), behind(COL_GDN_GATE // GDN_WIDTH)),
            pl.BlockSpec((1, ts, LANES), ahead(0)),
            pl.BlockSpec((kconv, qkv_w), lambda b, t: (0, 0)),
            pl.BlockSpec((1, LANES), lambda b, t: (0, 0)),
            pl.BlockSpec((1, LANES), lambda b, t: (0, 0)),
            pl.BlockSpec((1, GDN_HEAD_DIM), lambda b, t: (0, 0)),
        ],
        out_specs=pl.BlockSpec((1, ts, GDN_WIDTH), behind(0)),
        out_shape=jax.ShapeDtypeStruct((bsz, seq, GDN_WIDTH), BF16),
        scratch_shapes=[
            pltpu.VMEM((SUBLANES + ts, qkv_w), F32),
            pltpu.VMEM((GDN_HEADS, dk, dk), F32),
            pltpu.VMEM((nb, GDN_CHUNK, dk), F32),
            pltpu.VMEM((nb, GDN_CHUNK, dk), BF16),
            pltpu.VMEM((nb, GDN_CHUNK, GDN_CHUNK), BF16),
            pltpu.VMEM((nb, GDN_CHUNK, dk), BF16),
            pltpu.VMEM((nb, GDN_CHUNK, dk), F32),
            pltpu.VMEM((nb, SUBLANES, dk), F32),
        ],
        compiler_params=pltpu.CompilerParams(
            dimension_semantics=("parallel", "arbitrary"), vmem_limit_bytes=VMEM_LIMIT),
        name="gdn",
    )(proj3, proj3, proj3, ab3, conv_w, a_log_pad, dt_bias_pad, norm_w)


LOG2E = 1.4426950408889634
SB_EXIT = 104.0


def _softplus2(z2):
    neg_abs = lax.bitcast_convert_type(
        lax.bitcast_convert_type(z2, jnp.uint32) | jnp.uint32(0x80000000), F32)
    return jnp.maximum(z2, 0.0) + jnp.log2(1.0 + jnp.exp2(neg_abs))


def _sb_kernel(q_ref, k_ref, v_ref, o_ref, *, tq):
    qb = pl.program_id(2)
    pw = 2 * SB_HEAD_DIM
    q = q_ref[0] * jnp.asarray(SB_HEAD_DIM ** -0.5, BF16)
    first = lax.broadcasted_iota(jnp.int32, (tq, pw), 1) < SB_HEAD_DIM
    zero = jnp.zeros_like(q)
    qs = (jnp.where(first, q, zero), jnp.where(first, zero, q))
    row = lax.broadcasted_iota(jnp.int32, (tq, tq), 0)
    col = lax.broadcasted_iota(jnp.int32, (tq, tq), 1)
    upper = (row > col).astype(BF16)
    upper2 = jnp.concatenate([upper, upper], axis=0)
    causal = col < row
    heads = range(len(qs))

    def sweep(kvs, masks, gates, r_runs, accs):
        zs = [[_dot_nt(qh, k) * LOG2E for qh in qs] for k, _ in kvs]
        sps = [[_softplus2(z) for z in zb] for zb in zs]
        sps = [[sp if m is None else jnp.where(m, sp, 0.0) for sp in sb] for m, sb in zip(masks, sps)]
        afters = [[_dot(jnp.concatenate(_split2(sp), axis=1), upper2) for sp in sb] for sb in sps]
        tots = [[jnp.sum(sp, axis=-1, keepdims=True) for sp in sb] for sb in sps]
        r_runs = list(r_runs)
        accs = list(accs)
        for b, (_, v) in enumerate(kvs):
            for h in heads:
                att = jnp.exp2((zs[b][h] - sps[b][h]) - afters[b][h] - r_runs[h])
                if masks[b] is not None:
                    att = jnp.where(masks[b], att, 0.0)
                pv = _dot(att.astype(BF16), v)
                tot = tots[b][h]
                if gates[b] is not None:
                    pv = jnp.where(gates[b], pv, 0.0)
                    tot = jnp.where(gates[b], tot, 0.0)
                accs[h] = accs[h] + pv
                r_runs[h] = r_runs[h] + tot
        return r_runs, accs

    def kv(kb):
        k0 = pl.multiple_of(kb * tq, tq)
        return k_ref[0, pl.ds(k0, tq), :], v_ref[0, pl.ds(k0, tq), :]

    def done(r_runs):
        return jnp.min(jnp.minimum(r_runs[0], r_runs[1]))

    r_runs, accs = sweep([kv(qb), kv(jnp.maximum(qb - 1, 0))], [causal, None], [None, qb > 0],
                         [jnp.zeros((tq, 1), F32)] * 2, [jnp.zeros((tq, pw), F32)] * 2)

    def cond(c):
        return jnp.logical_and(c[0] <= qb, c[1] < SB_EXIT * LOG2E)

    def body(c):
        i, _, r_runs, accs = c
        r_runs, accs = sweep([kv(qb - i)], [None], [None], r_runs, accs)
        return i + 1, done(r_runs), r_runs, accs

    _, _, _, accs = lax.while_loop(cond, body, (jnp.int32(2), done(r_runs), r_runs, accs))
    o_ref[0] = jnp.where(first, accs[0], accs[1]).astype(o_ref.dtype)


def _sb(proj3):
    bsz, seq, _ = proj3.shape
    tq = min(256, seq)
    pw = 2 * SB_HEAD_DIM
    pairs = SB_HEADS // 2
    return pl.pallas_call(
        functools.partial(_sb_kernel, tq=tq),
        grid=(bsz, pairs, seq // tq),
        in_specs=[
            pl.BlockSpec((1, tq, pw), lambda b, p, i: (b, i, COL_SB_Q // pw + p)),
            pl.BlockSpec((1, seq, pw), lambda b, p, i: (b, 0, COL_SB_K // pw + p)),
            pl.BlockSpec((1, seq, pw), lambda b, p, i: (b, 0, COL_SB_V // pw + p)),
        ],
        out_specs=pl.BlockSpec((1, tq, pw), lambda b, p, i: (b, i, p)),
        out_shape=jax.ShapeDtypeStruct((bsz, seq, SB_WIDTH), BF16),
        compiler_params=pltpu.CompilerParams(
            dimension_semantics=("parallel", "parallel", "arbitrary"), vmem_limit_bytes=VMEM_LIMIT),
        name="stick_breaking",
    )(proj3, proj3, proj3)


def _merge_kernel(x_ref, ya_ref, yb_ref, scx_ref, scxh_ref, scc_ref, scch_ref, scb_ref,
                  g0_ref, g1_ref, g2_ref, convw_ref, wbr_ref, wout_ref, nw_ref, o_ref, cbuf,
                  *, tm, kconv):
    t = pl.program_id(1)
    halo = scxh_ref[0].astype(F32) * scch_ref[0].astype(F32)
    cbuf[0:SUBLANES, :] = jnp.where(t == 0, 0.0, halo)
    cbuf[SUBLANES:, :] = scx_ref[0].astype(F32) * scc_ref[0].astype(F32)
    conv = None
    for i in range(kconv):
        term = convw_ref[i:i + 1, :] * cbuf[pl.ds(SUBLANES - (kconv - 1) + i, tm), :]
        conv = term if conv is None else conv + term
    yc = scb_ref[0].astype(F32) * conv
    merged = (_sigmoid(g0_ref[0].astype(F32)) * _dot(ya_ref[0], wbr_ref[0])
              + _sigmoid(g1_ref[0].astype(F32)) * _dot(yb_ref[0], wbr_ref[1])
              + _sigmoid(g2_ref[0].astype(F32)) * _dot(yc.astype(BF16), wbr_ref[2]))
    r = _dot(merged.astype(BF16), wout_ref[...])
    o_ref[0] = x_ref[0] + _rms(r, nw_ref[...])


def _merge(x3, ya, yb, proj3, conv_w, w_br, w_out, nw):
    bsz, seq, d = x3.shape
    kconv = conv_w.shape[0]
    tm = min(512, seq)
    hb = tm // SUBLANES
    sw = SC_WIDTH

    def cur(colblk):
        return lambda b, t: (b, t, colblk)

    def prev(colblk):
        return lambda b, t: (b, jnp.maximum(t * hb - 1, 0), colblk)

    const2 = lambda b, t: (0, 0)
    return pl.pallas_call(
        functools.partial(_merge_kernel, tm=tm, kconv=kconv),
        grid=(bsz, seq // tm),
        in_specs=[
            pl.BlockSpec((1, tm, d), cur(0)),
            pl.BlockSpec((1, tm, GDN_WIDTH), cur(0)),
            pl.BlockSpec((1, tm, SB_WIDTH), cur(0)),
            pl.BlockSpec((1, tm, sw), cur(COL_SC_X // sw)),
            pl.BlockSpec((1, SUBLANES, sw), prev(COL_SC_X // sw)),
            pl.BlockSpec((1, tm, sw), cur(COL_SC_C // sw)),
            pl.BlockSpec((1, SUBLANES, sw), prev(COL_SC_C // sw)),
            pl.BlockSpec((1, tm, sw), cur(COL_SC_B // sw)),
            pl.BlockSpec((1, tm, d), cur(COL_GATES // d)),
            pl.BlockSpec((1, tm, d), cur(COL_GATES // d + 1)),
            pl.BlockSpec((1, tm, d), cur(COL_GATES // d + 2)),
            pl.BlockSpec((kconv, sw), const2),
            pl.BlockSpec((N_BRANCH, sw, d), lambda b, t: (0, 0, 0)),
            pl.BlockSpec((d, d), const2),
            pl.BlockSpec((1, d), const2),
        ],
        out_specs=pl.BlockSpec((1, tm, d), cur(0)),
        out_shape=jax.ShapeDtypeStruct((bsz, seq, d), F32),
        scratch_shapes=[pltpu.VMEM((SUBLANES + tm, sw), F32)],
        compiler_params=pltpu.CompilerParams(
            dimension_semantics=("parallel", "arbitrary"), vmem_limit_bytes=VMEM_LIMIT),
        name="merge",
    )(x3, ya, yb, proj3, proj3, proj3, proj3, proj3, proj3, proj3, proj3, conv_w, w_br, w_out, nw)


def _ffn_kernel(x_ref, nw1_ref, w1_ref, w2_ref, nw2_ref, o_ref, *, tf):
    x = x_ref[...]
    h = _rms(x, nw1_ref[...]).astype(BF16)
    g = None
    for c in range(w1_ref.shape[1] // tf):
        f = jnp.maximum(_dot(h, w1_ref[:, c * tf:(c + 1) * tf]), 0.0)
        part = _dot((f * f).astype(BF16), w2_ref[c * tf:(c + 1) * tf, :])
        g = part if g is None else g + part
    o_ref[...] = x + _rms(g, nw2_ref[...])


def _ffn(x2, nw1, w1, w2, nw2):
    tokens, d = x2.shape
    dff = w1.shape[1]
    tm = min(512, tokens)
    tf = min(1024, dff)
    resident = pl.Buffered(1)
    return pl.pallas_call(
        functools.partial(_ffn_kernel, tf=tf),
        grid=(tokens // tm,),
        in_specs=[
            pl.BlockSpec((tm, d), lambda i: (i, 0)),
            pl.BlockSpec((1, d), lambda i: (0, 0)),
            pl.BlockSpec((d, dff), lambda i: (0, 0), pipeline_mode=resident),
            pl.BlockSpec((dff, d), lambda i: (0, 0), pipeline_mode=resident),
            pl.BlockSpec((1, d), lambda i: (0, 0)),
        ],
        out_specs=pl.BlockSpec((tm, d), lambda i: (i, 0)),
        out_shape=jax.ShapeDtypeStruct((tokens, d), F32),
        compiler_params=pltpu.CompilerParams(
            dimension_semantics=("parallel",), vmem_limit_bytes=VMEM_LIMIT),
        name="ffn",
    )(x2, nw1, w1, w2, nw2)


def _split_w_in(w):
    gw, sw = GDN_WIDTH, SB_WIDTH
    o = 0
    gdn_qkv = w[:, o:o + 3 * gw]; o += 3 * gw
    gdn_gate = w[:, o:o + gw]; o += gw
    ab = w[:, o:o + 2 * GDN_HEADS]; o += 2 * GDN_HEADS
    sb_qkv = w[:, o:o + 3 * sw]; o += 3 * sw
    sc_x = w[:, o:o + SC_WIDTH]; o += SC_WIDTH
    sc_b = w[:, o:o + SC_WIDTH]; o += SC_WIDTH
    sc_c = w[:, o:o + SC_WIDTH]; o += SC_WIDTH
    gates = w[:, o:]
    main = jnp.concatenate([gdn_qkv, gdn_gate, sb_qkv, sc_x, sc_c, sc_b, gates], axis=1).astype(BF16)
    ab_pad = jnp.pad(ab, ((0, 0), (0, LANES - 2 * GDN_HEADS))).astype(BF16)
    return main, ab_pad


def _pad_lanes(v, offset=0):
    return jnp.pad(v.astype(F32), (offset, LANES - offset - v.shape[0]))[None, :]


def kernel(x, norm_mix_pre, w_in, conv_qkv_w, gdn_a_log, gdn_dt_bias, gdn_norm_w, conv_sc_w,
           w_branch, w_out, norm_mix_post, norm_ffn_pre, w_ff1, w_ff2, norm_ffn_post):
    bsz, seq, d = x.shape
    depth = w_in.shape[0]
    tokens = bsz * seq
    for l in range(depth):
        w_main, w_ab = _split_w_in(w_in[l])
        proj, ab = _in_proj(x.reshape(tokens, d), norm_mix_pre[l][None, :], w_main, w_ab)
        proj3 = proj.reshape(bsz, seq, -1)
        y_a = _gdn(proj3, ab.reshape(bsz, seq, LANES), conv_qkv_w[l], _pad_lanes(gdn_a_log[l]),
                   _pad_lanes(gdn_dt_bias[l]), gdn_norm_w[l][None, :])
        y_b = _sb(proj3)
        x = _merge(x, y_a, y_b, proj3, conv_sc_w[l], w_branch[l].astype(BF16), w_out[l].astype(BF16),
                   norm_mix_post[l][None, :])
        x = _ffn(x.reshape(tokens, d), norm_ffn_pre[l][None, :], w_ff1[l].astype(BF16),
                 w_ff2[l].astype(BF16), norm_ffn_post[l][None, :]).reshape(bsz, seq, d)
    return x
```

```python
import functools

import jax
import jax.numpy as jnp
from jax import lax
from jax.experimental import pallas as pl
from jax.experimental.pallas import tpu as pltpu

F32 = jnp.float32
BF16 = jnp.bfloat16
EPS = 1e-6

LANES = 128
SUBLANES = 8
VMEM_LIMIT = 56 * 1024 * 1024

GDN_HEADS = 4
GDN_HEAD_DIM = 128
GDN_WIDTH = GDN_HEADS * GDN_HEAD_DIM
GDN_CHUNK = 64
SB_HEADS = 8
SB_HEAD_DIM = 64
SB_WIDTH = SB_HEADS * SB_HEAD_DIM
SC_WIDTH = 512
N_BRANCH = 3

COL_GDN_QKV = 0
COL_GDN_GATE = 3 * GDN_WIDTH
COL_SB_Q = COL_GDN_GATE + GDN_WIDTH
COL_SB_K = COL_SB_Q + SB_WIDTH
COL_SB_V = COL_SB_K + SB_WIDTH
COL_SC_X = COL_SB_V + SB_WIDTH
COL_SC_C = COL_SC_X + SC_WIDTH
COL_SC_B = COL_SC_C + SC_WIDTH
COL_GATES = COL_SC_B + SC_WIDTH


def _sigmoid(x):
    return 0.5 * jnp.tanh(0.5 * x) + 0.5


def _softplus(x):
    return jnp.maximum(x, 0.0) + jnp.log(1.0 + jnp.exp(-jnp.abs(x)))


def _rms(x, w):
    return x * lax.rsqrt(jnp.mean(x * x, axis=-1, keepdims=True) + EPS) * w


def _dot(a, b):
    return jnp.dot(a, b, preferred_element_type=F32)


def _dot_nt(a, b):
    return lax.dot_general(a, b, (((1,), (1,)), ((), ())), preferred_element_type=F32)


def _split2(a):
    hi = a.astype(BF16)
    return hi, (a - hi.astype(F32)).astype(BF16)


def _inproj_kernel(x_ref, nw_ref, w_ref, wab_ref, o_ref, oab_ref, *, tn):
    hb = _rms(x_ref[...], nw_ref[...]).astype(BF16)
    oab_ref[...] = _dot(hb, wab_ref[...])
    for c in range(w_ref.shape[1] // tn):
        o_ref[:, c * tn:(c + 1) * tn] = _dot(hb, w_ref[:, c * tn:(c + 1) * tn]).astype(o_ref.dtype)


def _in_proj(x2, nw, w_main, w_ab):
    tokens, d = x2.shape
    n = w_main.shape[1]
    tm = min(512, tokens)
    tn = min(2048, n)
    resident = pl.Buffered(1)
    return pl.pallas_call(
        functools.partial(_inproj_kernel, tn=tn),
        grid=(tokens // tm,),
        in_specs=[
            pl.BlockSpec((tm, d), lambda i: (i, 0)),
            pl.BlockSpec((1, d), lambda i: (0, 0)),
            pl.BlockSpec((d, n), lambda i: (0, 0), pipeline_mode=resident),
            pl.BlockSpec((d, LANES), lambda i: (0, 0)),
        ],
        out_specs=[
            pl.BlockSpec((tm, n), lambda i: (i, 0)),
            pl.BlockSpec((tm, LANES), lambda i: (i, 0)),
        ],
        out_shape=[
            jax.ShapeDtypeStruct((tokens, n), BF16),
            jax.ShapeDtypeStruct((tokens, LANES), F32),
        ],
        compiler_params=pltpu.CompilerParams(
            dimension_semantics=("parallel",), vmem_limit_bytes=VMEM_LIMIT),
        name="in_proj",
    )(x2, nw, w_main, w_ab)


def _bmm(a, b):
    return jnp.einsum("bmk,bkn->bmn", a, b, preferred_element_type=F32)


def _bmm_nt(a, b):
    return jnp.einsum("bmk,bnk->bmn", a, b, preferred_element_type=F32)


def _gdn_kernel(qkv_ref, halo_ref, gate_ref, ab_ref, convw_ref, alog_ref, dtb_ref, nw_ref, o_ref,
                xbuf, state, u_buf, w_buf, aqk_buf, qdec_buf, kdec_buf, scale_buf, *, ts, kconv):
    t = pl.program_id(1)
    c_len = GDN_CHUNK
    dk = GDN_HEAD_DIM
    n_ch = ts // c_len

    @pl.when(t == 0)
    def _():
        state[...] = jnp.zeros_like(state)
        u_buf[...] = jnp.zeros_like(u_buf)
        w_buf[...] = jnp.zeros_like(w_buf)
        aqk_buf[...] = jnp.zeros_like(aqk_buf)
        qdec_buf[...] = jnp.zeros_like(qdec_buf)
        kdec_buf[...] = jnp.zeros_like(kdec_buf)
        scale_buf[...] = jnp.zeros_like(scale_buf)

    nw = nw_ref[...]
    s = [state[h] for h in range(GDN_HEADS)]
    pend = {}

    def scan_first(c, h):
        i = h * n_ch + c
        s16 = s[h].astype(BF16)
        pend[h] = (u_buf[i] - _dot(w_buf[i], s16), _dot(qdec_buf[i], s16))

    def scan_second(c, h):
        rows = slice(c * c_len, (c + 1) * c_len)
        i = h * n_ch + c
        v_new, o_state = pend[h]
        vn16 = v_new.astype(BF16)
        o = o_state + _dot(aqk_buf[i], vn16)
        s[h] = s[h] * scale_buf[i, 0:1, :] + _dot(kdec_buf[i].T.astype(BF16), vn16)
        o = o * lax.rsqrt(jnp.mean(o * o, axis=-1, keepdims=True) + EPS) * nw
        gate = gate_ref[0, rows, h * dk:(h + 1) * dk].astype(F32)
        o_ref[0, rows, h * dk:(h + 1) * dk] = (o * (gate * _sigmoid(gate))).astype(o_ref.dtype)

    half_steps = [functools.partial(f, c, h) for c in range(n_ch) for f in (scan_first, scan_second)
                  for h in range(GDN_HEADS)]

    def scan_advance(n=1):
        for _ in range(n):
            if half_steps:
                half_steps.pop(0)()

    xbuf[0:SUBLANES, :] = jnp.where(t == 0, 0.0, halo_ref[0].astype(F32))
    xbuf[SUBLANES:, :] = qkv_ref[0].astype(F32)
    acc = None
    for i in range(kconv):
        term = convw_ref[i:i + 1, :] * xbuf[pl.ds(SUBLANES - (kconv - 1) + i, ts), :]
        acc = term if acc is None else acc + term
    y = acc * _sigmoid(acc)

    ab = ab_ref[0]
    g_all = -jnp.exp(alog_ref[...]) * _softplus(ab + dtb_ref[...])
    beta_all = _sigmoid(ab)

    rt = lax.broadcasted_iota(jnp.int32, (ts, ts), 0)
    ct = lax.broadcasted_iota(jnp.int32, (ts, ts), 1)
    cum_mat = ((rt >= ct) & (rt // c_len == ct // c_len)).astype(BF16)
    g_hi = g_all.astype(BF16)
    g_r32 = g_all - g_hi.astype(F32)
    g_mid = g_r32.astype(BF16)
    g_lo = (g_r32 - g_mid.astype(F32)).astype(BF16)
    gcum = _dot(cum_mat, g_hi) + _dot(cum_mat, g_mid) + _dot(cum_mat, g_lo)

    def per_head(fn):
        return jnp.concatenate([fn(h) for h in range(GDN_HEADS)], axis=0)

    def head_cols(base):
        return per_head(lambda h: y[:, base + h * dk:base + (h + 1) * dk].reshape(n_ch, c_len, dk))

    q = head_cols(0)
    k = head_cols(GDN_WIDTH)
    v = head_cols(2 * GDN_WIDTH)
    q = q * lax.rsqrt(jnp.sum(q * q, axis=-1, keepdims=True) + EPS) * (dk ** -0.5)
    k = k * lax.rsqrt(jnp.sum(k * k, axis=-1, keepdims=True) + EPS)
    gc = per_head(lambda h: gcum[:, h:h + 1].reshape(n_ch, c_len, 1))
    beta = per_head(lambda h: beta_all[:, GDN_HEADS + h:GDN_HEADS + h + 1].reshape(n_ch, c_len, 1))
    gcum_t = [gcum[c * c_len:(c + 1) * c_len, :].T for c in range(n_ch)]
    gr = jnp.stack([gcum_t[c][h:h + 1, :] for h in range(GDN_HEADS) for c in range(n_ch)])
    g_end = gc[:, c_len - 1:c_len, :]

    row = lax.broadcasted_iota(jnp.int32, (c_len, c_len), 0)
    col = lax.broadcasted_iota(jnp.int32, (c_len, c_len), 1)
    tri = (row >= col)[None]
    strict = (row > col)[None]
    eye = (row == col).astype(F32)[None]

    decay = jnp.exp(jnp.where(tri, gc - gr, -1e30))
    egc = jnp.exp(gc)
    k_beta = k * beta
    k16 = k.astype(BF16)
    def bmm_s(a, b):
        scan_advance()
        return _bmm(a, b)

    scan_advance()
    a_kk = jnp.where(strict, _bmm_nt(k_beta.astype(BF16), k16) * decay, 0.0)
    p = -a_kk
    tinv = eye + p
    p_hi, p_lo = _split2(p)
    for _ in range(5):
        p = bmm_s(p_hi, p_hi) + (bmm_s(p_hi, p_lo) + bmm_s(p_lo, p_hi))
        p_hi, p_lo = _split2(p)
        t_hi, t_lo = _split2(tinv)
        tinv = tinv + (bmm_s(t_hi, p_hi) + (bmm_s(t_hi, p_lo) + bmm_s(t_lo, p_hi)))
    sol = bmm_s(tinv.astype(BF16), jnp.concatenate([v * beta, k_beta * egc], axis=-1).astype(BF16))
    a_qk = jnp.where(tri, _bmm_nt(q.astype(BF16), k16) * decay, 0.0)
    scan_advance(len(half_steps))
    for h in range(GDN_HEADS):
        state[h] = s[h]

    u_buf[...] = sol[:, :, :dk]
    w_buf[...] = sol[:, :, dk:].astype(BF16)
    aqk_buf[...] = a_qk.astype(BF16)
    qdec_buf[...] = (q * egc).astype(BF16)
    kdec_buf[...] = k * jnp.exp(g_end - gc)
    scale_buf[...] = jnp.broadcast_to(jnp.exp(g_end), scale_buf.shape)


def _gdn(proj3, ab3, conv_w, a_log_pad, dt_bias_pad, norm_w):
    bsz, seq, _ = proj3.shape
    kconv = conv_w.shape[0]
    ts = min(256, seq)
    qkv_w = 3 * GDN_WIDTH
    hb = ts // SUBLANES
    n_t = seq // ts
    nb = GDN_HEADS * (ts // GDN_CHUNK)
    dk = GDN_HEAD_DIM

    def ahead(colblk):
        return lambda b, t: (b, jnp.minimum(t, n_t - 1), colblk)

    def behind(colblk):
        return lambda b, t: (b, jnp.maximum(t - 1, 0), colblk)

    return pl.pallas_call(
        functools.partial(_gdn_kernel, ts=ts, kconv=kconv),
        grid=(bsz, n_t + 1),
        in_specs=[
            pl.BlockSpec((1, ts, qkv_w), ahead(COL_GDN_QKV // qkv_w)),
            pl.BlockSpec((1, SUBLANES, qkv_w),
                         lambda b, t: (b, jnp.maximum(jnp.minimum(t, n_t - 1) * hb - 1, 0), COL_GDN_QKV // qkv_w)),
            pl.BlockSpec((1, ts, GDN_WIDTH), behind(COL_GDN_GATE // GDN_WIDTH)),
            pl.BlockSpec((1, ts, LANES), ahead(0)),
            pl.BlockSpec((kconv, qkv_w), lambda b, t: (0, 0)),
            pl.BlockSpec((1, LANES), lambda b, t: (0, 0)),
            pl.BlockSpec((1, LANES), lambda b, t: (0, 0)),
            pl.BlockSpec((1, GDN_HEAD_DIM), lambda b, t: (0, 0)),
        ],
        out_specs=pl.BlockSpec((1, ts, GDN_WIDTH), behind(0)),
        out_shape=jax.ShapeDtypeStruct((bsz, seq, GDN_WIDTH), BF16),
        scratch_shapes=[
            pltpu.VMEM((SUBLANES + ts, qkv_w), F32),
            pltpu.VMEM((GDN_HEADS, dk, dk), F32),
            pltpu.VMEM((nb, GDN_CHUNK, dk), F32),
            pltpu.VMEM((nb, GDN_CHUNK, dk), BF16),
            pltpu.VMEM((nb, GDN_CHUNK, GDN_CHUNK), BF16),
            pltpu.VMEM((nb, GDN_CHUNK, dk), BF16),
            pltpu.VMEM((nb, GDN_CHUNK, dk), F32),
            pltpu.VMEM((nb, SUBLANES, dk), F32),
        ],
        compiler_params=pltpu.CompilerParams(
            dimension_semantics=("parallel", "arbitrary"), vmem_limit_bytes=VMEM_LIMIT),
        name="gdn",
    )(proj3, proj3, proj3, ab3, conv_w, a_log_pad, dt_bias_pad, norm_w)


LOG2E = 1.4426950408889634
SB_EXIT = 104.0


def _softplus2(z2):
    neg_abs = lax.bitcast_convert_type(
        lax.bitcast_convert_type(z2, jnp.uint32) | jnp.uint32(0x80000000), F32)
    return jnp.maximum(z2, 0.0) + jnp.log2(1.0 + jnp.exp2(neg_abs))


def _sb_kernel(q_ref, k_ref, v_ref, o_ref, *, tq, nq):
    step = pl.program_id(2)
    pw = 2 * SB_HEAD_DIM
    first = lax.broadcasted_iota(jnp.int32, (tq, pw), 1) < SB_HEAD_DIM
    qsets = []
    for j in range(nq):
        q = q_ref[0, j * tq:(j + 1) * tq, :] * jnp.asarray(SB_HEAD_DIM ** -0.5, BF16)
        zero = jnp.zeros_like(q)
        qsets.append((jnp.where(first, q, zero), jnp.where(first, zero, q)))
    qbs = [step * nq + j for j in range(nq)]
    row = lax.broadcasted_iota(jnp.int32, (tq, tq), 0)
    col = lax.broadcasted_iota(jnp.int32, (tq, tq), 1)
    upper = (row > col).astype(BF16)
    upper2 = jnp.concatenate([upper, upper], axis=0)
    causal = col < row
    heads = range(2)

    def sweep(items, r_runs, accs):
        zs = [[_dot_nt(qh, k) * LOG2E for qh in qsets[j]] for j, (k, _), _, _ in items]
        sps = [[_softplus2(z) for z in zb] for zb in zs]
        sps = [[sp if it[2] is None else jnp.where(it[2], sp, 0.0) for sp in sb] for it, sb in zip(items, sps)]
        afters = [[_dot(jnp.concatenate(_split2(sp), axis=1), upper2) for sp in sb] for sb in sps]
        tots = [[jnp.sum(sp, axis=-1, keepdims=True) for sp in sb] for sb in sps]
        r_runs = {j: list(r) for j, r in r_runs.items()}
        accs = {j: list(a) for j, a in accs.items()}
        for b, (j, (_, v), mask, gate) in enumerate(items):
            for h in heads:
                att = jnp.exp2((zs[b][h] - sps[b][h]) - afters[b][h] - r_runs[j][h])
                if mask is not None:
                    att = jnp.where(mask, att, 0.0)
                pv = _dot(att.astype(BF16), v)
                tot = tots[b][h]
                if gate is not None:
                    pv = jnp.where(gate, pv, 0.0)
                    tot = jnp.where(gate, tot, 0.0)
                accs[j][h] = accs[j][h] + pv
                r_runs[j][h] = r_runs[j][h] + tot
        return r_runs, accs

    def kv(kb):
        k0 = pl.multiple_of(kb * tq, tq)
        return k_ref[0, pl.ds(k0, tq), :], v_ref[0, pl.ds(k0, tq), :]

    def done(r):
        return jnp.min(jnp.minimum(r[0], r[1]))

    blocks = {}
    for kb_off in range(nq, -2, -1):
        blocks[kb_off] = kv(jnp.maximum(step * nq + kb_off, 0)) if kb_off < nq else None
    items = []
    for j in range(nq):
        items.append((j, blocks[j], causal, None))
        items.append((j, blocks[j - 1], None, (step > 0) if j == 0 else None))
    zr = [jnp.zeros((tq, 1), F32)] * 2
    za = [jnp.zeros((tq, pw), F32)] * 2
    r_runs, accs = sweep(items, {j: zr for j in range(nq)}, {j: za for j in range(nq)})

    for j in range(nq):
        qb = qbs[j]

        def cond(c, qb=qb):
            return jnp.logical_and(c[0] <= qb, c[1] < SB_EXIT * LOG2E)

        def body(c, j=j, qb=qb):
            i, _, r, a = c
            r_new, a_new = sweep([(j, kv(qb - i), None, None)], {j: r}, {j: a})
            return i + 1, done(r_new[j]), r_new[j], a_new[j]

        _, _, _, a = lax.while_loop(cond, body, (jnp.int32(2), done(r_runs[j]), r_runs[j], accs[j]))
        o_ref[0, j * tq:(j + 1) * tq, :] = jnp.where(first, a[0], a[1]).astype(o_ref.dtype)


def _sb(proj3):
    bsz, seq, _ = proj3.shape
    tq = min(256, seq)
    nq = 4 if seq % (4 * tq) == 0 else 1
    pw = 2 * SB_HEAD_DIM
    pairs = SB_HEADS // 2
    return pl.pallas_call(
        functools.partial(_sb_kernel, tq=tq, nq=nq),
        grid=(bsz, pairs, seq // (nq * tq)),
        in_specs=[
            pl.BlockSpec((1, nq * tq, pw), lambda b, p, i: (b, i, COL_SB_Q // pw + p)),
            pl.BlockSpec((1, seq, pw), lambda b, p, i: (b, 0, COL_SB_K // pw + p)),
            pl.BlockSpec((1, seq, pw), lambda b, p, i: (b, 0, COL_SB_V // pw + p)),
        ],
        out_specs=pl.BlockSpec((1, nq * tq, pw), lambda b, p, i: (b, i, p)),
        out_shape=jax.ShapeDtypeStruct((bsz, seq, SB_WIDTH), BF16),
        compiler_params=pltpu.CompilerParams(
            dimension_semantics=("parallel", "parallel", "arbitrary"), vmem_limit_bytes=VMEM_LIMIT),
        name="stick_breaking",
    )(proj3, proj3, proj3)


def _merge_kernel(x_ref, ya_ref, yb_ref, scx_ref, scxh_ref, scc_ref, scch_ref, scb_ref,
                  g0_ref, g1_ref, g2_ref, convw_ref, wbr_ref, wout_ref, nw_ref, o_ref, cbuf,
                  *, tm, kconv):
    t = pl.program_id(1)
    halo = scxh_ref[0].astype(F32) * scch_ref[0].astype(F32)
    cbuf[0:SUBLANES, :] = jnp.where(t == 0, 0.0, halo)
    cbuf[SUBLANES:, :] = scx_ref[0].astype(F32) * scc_ref[0].astype(F32)
    conv = None
    for i in range(kconv):
        term = convw_ref[i:i + 1, :] * cbuf[pl.ds(SUBLANES - (kconv - 1) + i, tm), :]
        conv = term if conv is None else conv + term
    yc = scb_ref[0].astype(F32) * conv
    merged = (_sigmoid(g0_ref[0].astype(F32)) * _dot(ya_ref[0], wbr_ref[0])
              + _sigmoid(g1_ref[0].astype(F32)) * _dot(yb_ref[0], wbr_ref[1])
              + _sigmoid(g2_ref[0].astype(F32)) * _dot(yc.astype(BF16), wbr_ref[2]))
    r = _dot(merged.astype(BF16), wout_ref[...])
    o_ref[0] = x_ref[0] + _rms(r, nw_ref[...])


def _merge(x3, ya, yb, proj3, conv_w, w_br, w_out, nw):
    bsz, seq, d = x3.shape
    kconv = conv_w.shape[0]
    tm = min(512, seq)
    hb = tm // SUBLANES
    sw = SC_WIDTH

    def cur(colblk):
        return lambda b, t: (b, t, colblk)

    def prev(colblk):
        return lambda b, t: (b, jnp.maximum(t * hb - 1, 0), colblk)

    const2 = lambda b, t: (0, 0)
    return pl.pallas_call(
        functools.partial(_merge_kernel, tm=tm, kconv=kconv),
        grid=(bsz, seq // tm),
        in_specs=[
            pl.BlockSpec((1, tm, d), cur(0)),
            pl.BlockSpec((1, tm, GDN_WIDTH), cur(0)),
            pl.BlockSpec((1, tm, SB_WIDTH), cur(0)),
            pl.BlockSpec((1, tm, sw), cur(COL_SC_X // sw)),
            pl.BlockSpec((1, SUBLANES, sw), prev(COL_SC_X // sw)),
            pl.BlockSpec((1, tm, sw), cur(COL_SC_C // sw)),
            pl.BlockSpec((1, SUBLANES, sw), prev(COL_SC_C // sw)),
            pl.BlockSpec((1, tm, sw), cur(COL_SC_B // sw)),
            pl.BlockSpec((1, tm, d), cur(COL_GATES // d)),
            pl.BlockSpec((1, tm, d), cur(COL_GATES // d + 1)),
            pl.BlockSpec((1, tm, d), cur(COL_GATES // d + 2)),
            pl.BlockSpec((kconv, sw), const2),
            pl.BlockSpec((N_BRANCH, sw, d), lambda b, t: (0, 0, 0)),
            pl.BlockSpec((d, d), const2),
            pl.BlockSpec((1, d), const2),
        ],
        out_specs=pl.BlockSpec((1, tm, d), cur(0)),
        out_shape=jax.ShapeDtypeStruct((bsz, seq, d), F32),
        scratch_shapes=[pltpu.VMEM((SUBLANES + tm, sw), F32)],
        compiler_params=pltpu.CompilerParams(
            dimension_semantics=("parallel", "arbitrary"), vmem_limit_bytes=VMEM_LIMIT),
        name="merge",
    )(x3, ya, yb, proj3, proj3, proj3, proj3, proj3, proj3, proj3, proj3, conv_w, w_br, w_out, nw)


def _ffn_kernel(x_ref, nw1_ref, w1_ref, w2_ref, nw2_ref, o_ref, *, tf):
    x = x_ref[...]
    h = _rms(x, nw1_ref[...]).astype(BF16)
    g = None
    for c in range(w1_ref.shape[1] // tf):
        f = jnp.maximum(_dot(h, w1_ref[:, c * tf:(c + 1) * tf]), 0.0)
        part = _dot((f * f).astype(BF16), w2_ref[c * tf:(c + 1) * tf, :])
        g = part if g is None else g + part
    o_ref[...] = x + _rms(g, nw2_ref[...])


def _ffn(x2, nw1, w1, w2, nw2):
    tokens, d = x2.shape
    dff = w1.shape[1]
    tm = min(512, tokens)
    tf = min(1024, dff)
    resident = pl.Buffered(1)
    return pl.pallas_call(
        functools.partial(_ffn_kernel, tf=tf),
        grid=(tokens // tm,),
        in_specs=[
            pl.BlockSpec((tm, d), lambda i: (i, 0)),
            pl.BlockSpec((1, d), lambda i: (0, 0)),
            pl.BlockSpec((d, dff), lambda i: (0, 0), pipeline_mode=resident),
            pl.BlockSpec((dff, d), lambda i: (0, 0), pipeline_mode=resident),
            pl.BlockSpec((1, d), lambda i: (0, 0)),
        ],
        out_specs=pl.BlockSpec((tm, d), lambda i: (i, 0)),
        out_shape=jax.ShapeDtypeStruct((tokens, d), F32),
        compiler_params=pltpu.CompilerParams(
            dimension_semantics=("parallel",), vmem_limit_bytes=VMEM_LIMIT),
        name="ffn",
    )(x2, nw1, w1, w2, nw2)


def _split_w_in(w):
    gw, sw = GDN_WIDTH, SB_WIDTH
    o = 0
    gdn_qkv = w[:, o:o + 3 * gw]; o += 3 * gw
    gdn_gate = w[:, o:o + gw]; o += gw
    ab = w[:, o:o + 2 * GDN_HEADS]; o += 2 * GDN_HEADS
    sb_qkv = w[:, o:o + 3 * sw]; o += 3 * sw
    sc_x = w[:, o:o + SC_WIDTH]; o += SC_WIDTH
    sc_b = w[:, o:o + SC_WIDTH]; o += SC_WIDTH
    sc_c = w[:, o:o + SC_WIDTH]; o += SC_WIDTH
    gates = w[:, o:]
    main = jnp.concatenate([gdn_qkv, gdn_gate, sb_qkv, sc_x, sc_c, sc_b, gates], axis=1).astype(BF16)
    ab_pad = jnp.pad(ab, ((0, 0), (0, LANES - 2 * GDN_HEADS))).astype(BF16)
    return main, ab_pad


def _pad_lanes(v, offset=0):
    return jnp.pad(v.astype(F32), (offset, LANES - offset - v.shape[0]))[None, :]


def kernel(x, norm_mix_pre, w_in, conv_qkv_w, gdn_a_log, gdn_dt_bias, gdn_norm_w, conv_sc_w,
           w_branch, w_out, norm_mix_post, norm_ffn_pre, w_ff1, w_ff2, norm_ffn_post):
    bsz, seq, d = x.shape
    depth = w_in.shape[0]
    tokens = bsz * seq
    for l in range(depth):
        w_main, w_ab = _split_w_in(w_in[l])
        proj, ab = _in_proj(x.reshape(tokens, d), norm_mix_pre[l][None, :], w_main, w_ab)
        proj3 = proj.reshape(bsz, seq, -1)
        y_a = _gdn(proj3, ab.reshape(bsz, seq, LANES), conv_qkv_w[l], _pad_lanes(gdn_a_log[l]),
                   _pad_lanes(gdn_dt_bias[l]), gdn_norm_w[l][None, :])
        y_b = _sb(proj3)
        x = _merge(x, y_a, y_b, proj3, conv_sc_w[l], w_branch[l].astype(BF16), w_out[l].astype(BF16),
                   norm_mix_post[l][None, :])
        x = _ffn(x.reshape(tokens, d), norm_ffn_pre[l][None, :], w_ff1[l].astype(BF16),
                 w_ff2[l].astype(BF16), norm_ffn_post[l][None, :]).reshape(bsz, seq, d)
    return x
```

```python
import functools

import jax
import jax.numpy as jnp
from jax import lax
from jax.experimental import pallas as pl
from jax.experimental.pallas import tpu as pltpu

F32 = jnp.float32
BF16 = jnp.bfloat16
EPS = 1e-6

LANES = 128
SUBLANES = 8
VMEM_LIMIT = 56 * 1024 * 1024

GDN_HEADS = 4
GDN_HEAD_DIM = 128
GDN_WIDTH = GDN_HEADS * GDN_HEAD_DIM
GDN_CHUNK = 64
SB_HEADS = 8
SB_HEAD_DIM = 64
SB_WIDTH = SB_HEADS * SB_HEAD_DIM
SC_WIDTH = 512
N_BRANCH = 3

COL_GDN_QKV = 0
COL_GDN_GATE = 3 * GDN_WIDTH
COL_SB_Q = COL_GDN_GATE + GDN_WIDTH
COL_SB_K = COL_SB_Q + SB_WIDTH
COL_SB_V = COL_SB_K + SB_WIDTH
COL_SC_X = COL_SB_V + SB_WIDTH
COL_SC_C = COL_SC_X + SC_WIDTH
COL_SC_B = COL_SC_C + SC_WIDTH
COL_GATES = COL_SC_B + SC_WIDTH


def _sigmoid(x):
    return 0.5 * jnp.tanh(0.5 * x) + 0.5


def _softplus(x):
    return jnp.maximum(x, 0.0) + jnp.log(1.0 + jnp.exp(-jnp.abs(x)))


def _rms(x, w):
    return x * lax.rsqrt(jnp.mean(x * x, axis=-1, keepdims=True) + EPS) * w


def _dot(a, b):
    return jnp.dot(a, b, preferred_element_type=F32)


def _dot_nt(a, b):
    return lax.dot_general(a, b, (((1,), (1,)), ((), ())), preferred_element_type=F32)


def _split2(a):
    hi = a.astype(BF16)
    return hi, (a - hi.astype(F32)).astype(BF16)


def _layer_spec(shape, l, **kw):
    return pl.BlockSpec((None,) + tuple(shape), lambda *_: (l,) + (0,) * len(shape), **kw)


def _inproj_kernel(x_ref, nw_ref, w_ref, wab_ref, o_ref, oab_ref, *, tn):
    hb = _rms(x_ref[...], nw_ref[...]).astype(BF16)
    oab_ref[...] = _dot(hb, wab_ref[...])
    for c in range(w_ref.shape[1] // tn):
        o_ref[:, c * tn:(c + 1) * tn] = _dot(hb, w_ref[:, c * tn:(c + 1) * tn]).astype(o_ref.dtype)


def _in_proj(x2, nw, w_main, w_ab, l):
    tokens, d = x2.shape
    n = w_main.shape[2]
    tm = min(512, tokens)
    tn = min(2048, n)
    resident = pl.Buffered(1)
    return pl.pallas_call(
        functools.partial(_inproj_kernel, tn=tn),
        grid=(tokens // tm,),
        in_specs=[
            pl.BlockSpec((tm, d), lambda i: (i, 0)),
            _layer_spec((1, d), l),
            _layer_spec((d, n), l, pipeline_mode=resident),
            _layer_spec((d, LANES), l),
        ],
        out_specs=[
            pl.BlockSpec((tm, n), lambda i: (i, 0)),
            pl.BlockSpec((tm, LANES), lambda i: (i, 0)),
        ],
        out_shape=[
            jax.ShapeDtypeStruct((tokens, n), BF16),
            jax.ShapeDtypeStruct((tokens, LANES), F32),
        ],
        compiler_params=pltpu.CompilerParams(
            dimension_semantics=("parallel",), vmem_limit_bytes=VMEM_LIMIT),
        name="in_proj",
    )(x2, nw, w_main, w_ab)


def _bmm(a, b):
    return jnp.einsum("bmk,bkn->bmn", a, b, preferred_element_type=F32)


def _bmm_nt(a, b):
    return jnp.einsum("bmk,bnk->bmn", a, b, preferred_element_type=F32)


def _gdn_kernel(*refs, ts, kconv, n_t):
    t = pl.program_id(1)
    state = refs[10]

    @pl.when(t == 0)
    def _():
        state[...] = jnp.zeros_like(state)
        _gdn_region(*refs, ts=ts, kconv=kconv, do_scan=False, do_intra=True)

    @pl.when(jnp.logical_and(t > 0, t < n_t))
    def _():
        _gdn_region(*refs, ts=ts, kconv=kconv, do_scan=True, do_intra=True)

    @pl.when(t == n_t)
    def _():
        _gdn_region(*refs, ts=ts, kconv=kconv, do_scan=True, do_intra=False)


def _gdn_region(qkv_ref, halo_ref, gate_ref, ab_ref, convw_ref, alog_ref, dtb_ref, nw_ref, o_ref,
                xbuf, state, u_buf, w_buf, aqk_buf, qdec_buf, kdec_buf, scale_buf,
                *, ts, kconv, do_scan, do_intra):
    t = pl.program_id(1)
    c_len = GDN_CHUNK
    dk = GDN_HEAD_DIM
    n_ch = ts // c_len

    nw = nw_ref[...]
    s = [state[h] for h in range(GDN_HEADS)]
    pend = {}

    def scan_first(c, h):
        i = h * n_ch + c
        s16 = s[h].astype(BF16)
        pend[h] = (u_buf[i] - _dot(w_buf[i], s16), _dot(qdec_buf[i], s16))

    def scan_second(c, h):
        rows = slice(c * c_len, (c + 1) * c_len)
        i = h * n_ch + c
        v_new, o_state = pend[h]
        vn16 = v_new.astype(BF16)
        o = o_state + _dot(aqk_buf[i], vn16)
        s[h] = s[h] * scale_buf[i, 0:1, :] + _dot(kdec_buf[i].T.astype(BF16), vn16)
        o = o * lax.rsqrt(jnp.mean(o * o, axis=-1, keepdims=True) + EPS) * nw
        gate = gate_ref[0, rows, h * dk:(h + 1) * dk].astype(F32)
        o_ref[0, rows, h * dk:(h + 1) * dk] = (o * (gate * _sigmoid(gate))).astype(o_ref.dtype)

    half_steps = [functools.partial(f, c, h) for c in range(n_ch) for f in (scan_first, scan_second)
                  for h in range(GDN_HEADS)] if do_scan else []

    def scan_advance(n=1):
        for _ in range(n):
            if half_steps:
                half_steps.pop(0)()

    def scan_finish():
        scan_advance(len(half_steps))
        if do_scan:
            for h in range(GDN_HEADS):
                state[h] = s[h]

    if not do_intra:
        scan_finish()
        return

    xbuf[0:SUBLANES, :] = jnp.where(t == 0, 0.0, halo_ref[0].astype(F32))
    xbuf[SUBLANES:, :] = qkv_ref[0].astype(F32)
    acc = None
    for i in range(kconv):
        term = convw_ref[i:i + 1, :] * xbuf[pl.ds(SUBLANES - (kconv - 1) + i, ts), :]
        acc = term if acc is None else acc + term
    y = acc * _sigmoid(acc)

    ab = ab_ref[0]
    g_all = -jnp.exp(alog_ref[...]) * _softplus(ab + dtb_ref[...])
    beta_all = _sigmoid(ab)

    rt = lax.broadcasted_iota(jnp.int32, (ts, ts), 0)
    ct = lax.broadcasted_iota(jnp.int32, (ts, ts), 1)
    cum_mat = ((rt >= ct) & (rt // c_len == ct // c_len)).astype(BF16)
    g_hi = g_all.astype(BF16)
    g_r32 = g_all - g_hi.astype(F32)
    g_mid = g_r32.astype(BF16)
    g_lo = (g_r32 - g_mid.astype(F32)).astype(BF16)
    gcum = _dot(cum_mat, g_hi) + _dot(cum_mat, g_mid) + _dot(cum_mat, g_lo)

    def per_head(fn):
        return jnp.concatenate([fn(h) for h in range(GDN_HEADS)], axis=0)

    def head_cols(base):
        return per_head(lambda h: y[:, base + h * dk:base + (h + 1) * dk].reshape(n_ch, c_len, dk))

    q = head_cols(0)
    k = head_cols(GDN_WIDTH)
    v = head_cols(2 * GDN_WIDTH)
    q = q * lax.rsqrt(jnp.sum(q * q, axis=-1, keepdims=True) + EPS) * (dk ** -0.5)
    k = k * lax.rsqrt(jnp.sum(k * k, axis=-1, keepdims=True) + EPS)
    gc = per_head(lambda h: gcum[:, h:h + 1].reshape(n_ch, c_len, 1))
    beta = per_head(lambda h: beta_all[:, GDN_HEADS + h:GDN_HEADS + h + 1].reshape(n_ch, c_len, 1))
    gcum_t = [gcum[c * c_len:(c + 1) * c_len, :].T for c in range(n_ch)]
    gr = jnp.stack([gcum_t[c][h:h + 1, :] for h in range(GDN_HEADS) for c in range(n_ch)])
    g_end = gc[:, c_len - 1:c_len, :]

    row = lax.broadcasted_iota(jnp.int32, (c_len, c_len), 0)
    col = lax.broadcasted_iota(jnp.int32, (c_len, c_len), 1)
    tri = (row >= col)[None]
    strict = (row > col)[None]
    eye = (row == col).astype(F32)[None]

    decay = jnp.exp(jnp.where(tri, gc - gr, -1e30))
    egc = jnp.exp(gc)
    k_beta = k * beta
    k16 = k.astype(BF16)
    def bmm_s(a, b):
        scan_advance()
        return _bmm(a, b)

    scan_advance()
    a_kk = jnp.where(strict, _bmm_nt(k_beta.astype(BF16), k16) * decay, 0.0)
    p = -a_kk
    tinv = eye + p
    p_hi, p_lo = _split2(p)
    for _ in range(5):
        p = bmm_s(p_hi, p_hi) + (bmm_s(p_hi, p_lo) + bmm_s(p_lo, p_hi))
        p_hi, p_lo = _split2(p)
        t_hi, t_lo = _split2(tinv)
        tinv = tinv + (bmm_s(t_hi, p_hi) + (bmm_s(t_hi, p_lo) + bmm_s(t_lo, p_hi)))
    sol = bmm_s(tinv.astype(BF16), jnp.concatenate([v * beta, k_beta * egc], axis=-1).astype(BF16))
    a_qk = jnp.where(tri, _bmm_nt(q.astype(BF16), k16) * decay, 0.0)
    scan_finish()

    u_buf[...] = sol[:, :, :dk]
    w_buf[...] = sol[:, :, dk:].astype(BF16)
    aqk_buf[...] = a_qk.astype(BF16)
    qdec_buf[...] = (q * egc).astype(BF16)
    kdec_buf[...] = k * jnp.exp(g_end - gc)
    scale_buf[...] = jnp.broadcast_to(jnp.exp(g_end), scale_buf.shape)


def _gdn(proj3, ab3, conv_w, a_log_pad, dt_bias_pad, norm_w, l):
    bsz, seq, _ = proj3.shape
    kconv = conv_w.shape[1]
    ts = min(256, seq)
    qkv_w = 3 * GDN_WIDTH
    hb = ts // SUBLANES
    n_t = seq // ts
    nb = GDN_HEADS * (ts // GDN_CHUNK)
    dk = GDN_HEAD_DIM

    def ahead(colblk):
        return lambda b, t: (b, jnp.minimum(t, n_t - 1), colblk)

    def behind(colblk):
        return lambda b, t: (b, jnp.maximum(t - 1, 0), colblk)

    return pl.pallas_call(
        functools.partial(_gdn_kernel, ts=ts, kconv=kconv, n_t=n_t),
        grid=(bsz, n_t + 1),
        in_specs=[
            pl.BlockSpec((1, ts, qkv_w), ahead(COL_GDN_QKV // qkv_w)),
            pl.BlockSpec((1, SUBLANES, qkv_w),
                         lambda b, t: (b, jnp.maximum(jnp.minimum(t, n_t - 1) * hb - 1, 0), COL_GDN_QKV // qkv_w)),
            pl.BlockSpec((1, ts, GDN_WIDTH), behind(COL_GDN_GATE // GDN_WIDTH)),
            pl.BlockSpec((1, ts, LANES), ahead(0)),
            _layer_spec((kconv, qkv_w), l),
            _layer_spec((1, LANES), l),
            _layer_spec((1, LANES), l),
            _layer_spec((1, GDN_HEAD_DIM), l),
        ],
        out_specs=pl.BlockSpec((1, ts, GDN_WIDTH), behind(0)),
        out_shape=jax.ShapeDtypeStruct((bsz, seq, GDN_WIDTH), BF16),
        scratch_shapes=[
            pltpu.VMEM((SUBLANES + ts, qkv_w), F32),
            pltpu.VMEM((GDN_HEADS, dk, dk), F32),
            pltpu.VMEM((nb, GDN_CHUNK, dk), F32),
            pltpu.VMEM((nb, GDN_CHUNK, dk), BF16),
            pltpu.VMEM((nb, GDN_CHUNK, GDN_CHUNK), BF16),
            pltpu.VMEM((nb, GDN_CHUNK, dk), BF16),
            pltpu.VMEM((nb, GDN_CHUNK, dk), F32),
            pltpu.VMEM((nb, SUBLANES, dk), F32),
        ],
        compiler_params=pltpu.CompilerParams(
            dimension_semantics=("parallel", "arbitrary"), vmem_limit_bytes=VMEM_LIMIT),
        name="gdn",
    )(proj3, proj3, proj3, ab3, conv_w, a_log_pad, dt_bias_pad, norm_w)


LOG2E = 1.4426950408889634
SB_EXIT = 104.0


def _softplus2(z2):
    neg_abs = lax.bitcast_convert_type(
        lax.bitcast_convert_type(z2, jnp.uint32) | jnp.uint32(0x80000000), F32)
    return jnp.maximum(z2, 0.0) + jnp.log2(1.0 + jnp.exp2(neg_abs))


def _sb_kernel(q_ref, k_ref, v_ref, o_ref, *, tq, nq):
    step = pl.program_id(2)
    pw = 2 * SB_HEAD_DIM
    first = lax.broadcasted_iota(jnp.int32, (tq, pw), 1) < SB_HEAD_DIM
    qsets = []
    for j in range(nq):
        q = q_ref[0, j * tq:(j + 1) * tq, :] * jnp.asarray(SB_HEAD_DIM ** -0.5, BF16)
        zero = jnp.zeros_like(q)
        qsets.append((jnp.where(first, q, zero), jnp.where(first, zero, q)))
    qbs = [step * nq + j for j in range(nq)]
    row = lax.broadcasted_iota(jnp.int32, (tq, tq), 0)
    col = lax.broadcasted_iota(jnp.int32, (tq, tq), 1)
    upper = (row > col).astype(BF16)
    upper2 = jnp.concatenate([upper, upper], axis=0)
    causal = col < row
    heads = range(2)

    def sweep(items, r_runs, accs):
        zs = [[_dot_nt(qh, k) * LOG2E for qh in qsets[j]] for j, (k, _), _, _ in items]
        sps = [[_softplus2(z) for z in zb] for zb in zs]
        sps = [[sp if it[2] is None else jnp.where(it[2], sp, 0.0) for sp in sb] for it, sb in zip(items, sps)]
        afters = [[_dot(jnp.concatenate(_split2(sp), axis=1), upper2) for sp in sb] for sb in sps]
        tots = [[jnp.sum(sp, axis=-1, keepdims=True) for sp in sb] for sb in sps]
        r_runs = {j: list(r) for j, r in r_runs.items()}
        accs = {j: list(a) for j, a in accs.items()}
        for b, (j, (_, v), mask, gate) in enumerate(items):
            for h in heads:
                att = jnp.exp2((zs[b][h] - sps[b][h]) - afters[b][h] - r_runs[j][h])
                if mask is not None:
                    att = jnp.where(mask, att, 0.0)
                pv = _dot(att.astype(BF16), v)
                tot = tots[b][h]
                if gate is not None:
                    pv = jnp.where(gate, pv, 0.0)
                    tot = jnp.where(gate, tot, 0.0)
                accs[j][h] = accs[j][h] + pv
                r_runs[j][h] = r_runs[j][h] + tot
        return r_runs, accs

    def kv(kb):
        k0 = pl.multiple_of(kb * tq, tq)
        return k_ref[0, pl.ds(k0, tq), :], v_ref[0, pl.ds(k0, tq), :]

    def done(r):
        return jnp.min(jnp.minimum(r[0], r[1]))

    blocks = {}
    for kb_off in range(nq, -2, -1):
        blocks[kb_off] = kv(jnp.maximum(step * nq + kb_off, 0)) if kb_off < nq else None
    items = []
    for j in range(nq):
        items.append((j, blocks[j], causal, None))
        items.append((j, blocks[j - 1], None, (step > 0) if j == 0 else None))
    zr = [jnp.zeros((tq, 1), F32)] * 2
    za = [jnp.zeros((tq, pw), F32)] * 2
    r_runs, accs = sweep(items, {j: zr for j in range(nq)}, {j: za for j in range(nq)})

    for j in range(nq):
        qb = qbs[j]

        def cond(c, qb=qb):
            return jnp.logical_and(c[0] <= qb, c[1] < SB_EXIT * LOG2E)

        def body(c, j=j, qb=qb):
            i, _, r, a = c
            r_new, a_new = sweep([(j, kv(qb - i), None, None)], {j: r}, {j: a})
            return i + 1, done(r_new[j]), r_new[j], a_new[j]

        _, _, _, a = lax.while_loop(cond, body, (jnp.int32(2), done(r_runs[j]), r_runs[j], accs[j]))
        o_ref[0, j * tq:(j + 1) * tq, :] = jnp.where(first, a[0], a[1]).astype(o_ref.dtype)


def _sb(proj3):
    bsz, seq, _ = proj3.shape
    tq = min(256, seq)
    nq = 4 if seq % (4 * tq) == 0 else 1
    pw = 2 * SB_HEAD_DIM
    pairs = SB_HEADS // 2
    return pl.pallas_call(
        functools.partial(_sb_kernel, tq=tq, nq=nq),
        grid=(bsz, pairs, seq // (nq * tq)),
        in_specs=[
            pl.BlockSpec((1, nq * tq, pw), lambda b, p, i: (b, i, COL_SB_Q // pw + p)),
            pl.BlockSpec((1, seq, pw), lambda b, p, i: (b, 0, COL_SB_K // pw + p)),
            pl.BlockSpec((1, seq, pw), lambda b, p, i: (b, 0, COL_SB_V // pw + p)),
        ],
        out_specs=pl.BlockSpec((1, nq * tq, pw), lambda b, p, i: (b, i, p)),
        out_shape=jax.ShapeDtypeStruct((bsz, seq, SB_WIDTH), BF16),
        compiler_params=pltpu.CompilerParams(
            dimension_semantics=("parallel", "parallel", "arbitrary"), vmem_limit_bytes=VMEM_LIMIT),
        name="stick_breaking",
    )(proj3, proj3, proj3)


def _merge_kernel(x_ref, ya_ref, yb_ref, scx_ref, scxh_ref, scc_ref, scch_ref, scb_ref,
                  g0_ref, g1_ref, g2_ref, convw_ref, wbr_ref, wout_ref, nw_ref, o_ref, cbuf,
                  *, tm, kconv):
    t = pl.program_id(1)
    halo = scxh_ref[0].astype(F32) * scch_ref[0].astype(F32)
    cbuf[0:SUBLANES, :] = jnp.where(t == 0, 0.0, halo)
    cbuf[SUBLANES:, :] = scx_ref[0].astype(F32) * scc_ref[0].astype(F32)
    conv = None
    for i in range(kconv):
        term = convw_ref[i:i + 1, :] * cbuf[pl.ds(SUBLANES - (kconv - 1) + i, tm), :]
        conv = term if conv is None else conv + term
    yc = scb_ref[0].astype(F32) * conv
    merged = (_sigmoid(g0_ref[0].astype(F32)) * _dot(ya_ref[0], wbr_ref[0])
              + _sigmoid(g1_ref[0].astype(F32)) * _dot(yb_ref[0], wbr_ref[1])
              + _sigmoid(g2_ref[0].astype(F32)) * _dot(yc.astype(BF16), wbr_ref[2]))
    r = _dot(merged.astype(BF16), wout_ref[...])
    o_ref[0] = x_ref[0] + _rms(r, nw_ref[...])


def _merge(x3, ya, yb, proj3, conv_w, w_br, w_out, nw, l):
    bsz, seq, d = x3.shape
    kconv = conv_w.shape[1]
    tm = min(512, seq)
    hb = tm // SUBLANES
    sw = SC_WIDTH

    def cur(colblk):
        return lambda b, t: (b, t, colblk)

    def prev(colblk):
        return lambda b, t: (b, jnp.maximum(t * hb - 1, 0), colblk)

    return pl.pallas_call(
        functools.partial(_merge_kernel, tm=tm, kconv=kconv),
        grid=(bsz, seq // tm),
        in_specs=[
            pl.BlockSpec((1, tm, d), cur(0)),
            pl.BlockSpec((1, tm, GDN_WIDTH), cur(0)),
            pl.BlockSpec((1, tm, SB_WIDTH), cur(0)),
            pl.BlockSpec((1, tm, sw), cur(COL_SC_X // sw)),
            pl.BlockSpec((1, SUBLANES, sw), prev(COL_SC_X // sw)),
            pl.BlockSpec((1, tm, sw), cur(COL_SC_C // sw)),
            pl.BlockSpec((1, SUBLANES, sw), prev(COL_SC_C // sw)),
            pl.BlockSpec((1, tm, sw), cur(COL_SC_B // sw)),
            pl.BlockSpec((1, tm, d), cur(COL_GATES // d)),
            pl.BlockSpec((1, tm, d), cur(COL_GATES // d + 1)),
            pl.BlockSpec((1, tm, d), cur(COL_GATES // d + 2)),
            _layer_spec((kconv, sw), l),
            _layer_spec((N_BRANCH, sw, d), l),
            _layer_spec((d, d), l),
            _layer_spec((1, d), l),
        ],
        out_specs=pl.BlockSpec((1, tm, d), cur(0)),
        out_shape=jax.ShapeDtypeStruct((bsz, seq, d), F32),
        scratch_shapes=[pltpu.VMEM((SUBLANES + tm, sw), F32)],
        compiler_params=pltpu.CompilerParams(
            dimension_semantics=("parallel", "arbitrary"), vmem_limit_bytes=VMEM_LIMIT),
        name="merge",
    )(x3, ya, yb, proj3, proj3, proj3, proj3, proj3, proj3, proj3, proj3, conv_w, w_br, w_out, nw)


def _ffn_kernel(x_ref, nw1_ref, w1_ref, w2_ref, nw2_ref, o_ref, *, tf):
    x = x_ref[...]
    h = _rms(x, nw1_ref[...]).astype(BF16)
    g = None
    for c in range(w1_ref.shape[1] // tf):
        f = jnp.maximum(_dot(h, w1_ref[:, c * tf:(c + 1) * tf]), 0.0)
        part = _dot((f * f).astype(BF16), w2_ref[c * tf:(c + 1) * tf, :])
        g = part if g is None else g + part
    o_ref[...] = x + _rms(g, nw2_ref[...])


def _ffn(x2, nw1, w1, w2, nw2, l):
    tokens, d = x2.shape
    dff = w1.shape[2]
    tm = min(512, tokens)
    tf = min(1024, dff)
    resident = pl.Buffered(1)
    return pl.pallas_call(
        functools.partial(_ffn_kernel, tf=tf),
        grid=(tokens // tm,),
        in_specs=[
            pl.BlockSpec((tm, d), lambda i: (i, 0)),
            _layer_spec((1, d), l),
            _layer_spec((d, dff), l, pipeline_mode=resident),
            _layer_spec((dff, d), l, pipeline_mode=resident),
            _layer_spec((1, d), l),
        ],
        out_specs=pl.BlockSpec((tm, d), lambda i: (i, 0)),
        out_shape=jax.ShapeDtypeStruct((tokens, d), F32),
        compiler_params=pltpu.CompilerParams(
            dimension_semantics=("parallel",), vmem_limit_bytes=VMEM_LIMIT),
        name="ffn",
    )(x2, nw1, w1, w2, nw2)


def _split_w_in(w):
    gw, sw = GDN_WIDTH, SB_WIDTH
    o = 0
    gdn_qkv = w[..., o:o + 3 * gw]; o += 3 * gw
    gdn_gate = w[..., o:o + gw]; o += gw
    ab = w[..., o:o + 2 * GDN_HEADS]; o += 2 * GDN_HEADS
    sb_qkv = w[..., o:o + 3 * sw]; o += 3 * sw
    sc_x = w[..., o:o + SC_WIDTH]; o += SC_WIDTH
    sc_b = w[..., o:o + SC_WIDTH]; o += SC_WIDTH
    sc_c = w[..., o:o + SC_WIDTH]; o += SC_WIDTH
    gates = w[..., o:]
    main = jnp.concatenate([gdn_qkv, gdn_gate, sb_qkv, sc_x, sc_c, sc_b, gates], axis=-1).astype(BF16)
    ab_pad = jnp.pad(ab, ((0, 0), (0, 0), (0, LANES - 2 * GDN_HEADS))).astype(BF16)
    return main, ab_pad


def _pad_lanes(v):
    return jnp.pad(v.astype(F32), ((0, 0), (0, LANES - v.shape[1])))[:, None, :]


def kernel(x, norm_mix_pre, w_in, conv_qkv_w, gdn_a_log, gdn_dt_bias, gdn_norm_w, conv_sc_w,
           w_branch, w_out, norm_mix_post, norm_ffn_pre, w_ff1, w_ff2, norm_ffn_post):
    bsz, seq, d = x.shape
    depth = w_in.shape[0]
    tokens = bsz * seq
    w_main, w_ab = _split_w_in(w_in)
    a_log_pad, dt_bias_pad = _pad_lanes(gdn_a_log), _pad_lanes(gdn_dt_bias)
    w_br16, w_out16, w_ff1_16, w_ff2_16 = (w.astype(BF16) for w in (w_branch, w_out, w_ff1, w_ff2))
    row = lambda v: v[:, None, :]
    for l in range(depth):
        proj, ab = _in_proj(x.reshape(tokens, d), row(norm_mix_pre), w_main, w_ab, l)
        proj3 = proj.reshape(bsz, seq, -1)
        y_a = _gdn(proj3, ab.reshape(bsz, seq, LANES), conv_qkv_w, a_log_pad, dt_bias_pad, row(gdn_norm_w), l)
        y_b = _sb(proj3)
        x = _merge(x, y_a, y_b, proj3, conv_sc_w, w_br16, w_out16, row(norm_mix_post), l)
        x = _ffn(x.reshape(tokens, d), row(norm_ffn_pre), w_ff1_16, w_ff2_16, row(norm_ffn_post),
                 l).reshape(bsz, seq, d)
    return x
```

```python
import functools

import jax
import jax.numpy as jnp
from jax import lax
from jax.experimental import pallas as pl
from jax.experimental.pallas import tpu as pltpu

F32 = jnp.float32
BF16 = jnp.bfloat16
EPS = 1e-6

LANES = 128
SUBLANES = 8
VMEM_LIMIT = 56 * 1024 * 1024

GDN_HEADS = 4
GDN_HEAD_DIM = 128
GDN_WIDTH = GDN_HEADS * GDN_HEAD_DIM
GDN_CHUNK = 64
SB_HEADS = 8
SB_HEAD_DIM = 64
SB_WIDTH = SB_HEADS * SB_HEAD_DIM
SC_WIDTH = 512
N_BRANCH = 3

COL_GDN_QKV = 0
COL_GDN_GATE = 3 * GDN_WIDTH
COL_SB_Q = COL_GDN_GATE + GDN_WIDTH
COL_SB_K = COL_SB_Q + SB_WIDTH
COL_SB_V = COL_SB_K + SB_WIDTH
COL_SC_X = COL_SB_V + SB_WIDTH
COL_SC_C = COL_SC_X + SC_WIDTH
COL_SC_B = COL_SC_C + SC_WIDTH
COL_GATES = COL_SC_B + SC_WIDTH


def _sigmoid(x):
    return 0.5 * jnp.tanh(0.5 * x) + 0.5


def _softplus(x):
    return jnp.maximum(x, 0.0) + jnp.log(1.0 + jnp.exp(-jnp.abs(x)))


def _rms(x, w):
    return x * lax.rsqrt(jnp.mean(x * x, axis=-1, keepdims=True) + EPS) * w


def _dot(a, b):
    return jnp.dot(a, b, preferred_element_type=F32)


def _dot_nt(a, b):
    return lax.dot_general(a, b, (((1,), (1,)), ((), ())), preferred_element_type=F32)


def _split2(a):
    hi = a.astype(BF16)
    return hi, (a - hi.astype(F32)).astype(BF16)


def _layer_spec(shape, l, **kw):
    return pl.BlockSpec((None,) + tuple(shape), lambda *_: (l,) + (0,) * len(shape), **kw)


def _inproj_kernel(x_ref, nw_ref, w_ref, wab_ref, o_ref, oab_ref, *, tn):
    hb = _rms(x_ref[...], nw_ref[...]).astype(BF16)
    oab_ref[...] = _dot(hb, wab_ref[...])
    for c in range(w_ref.shape[1] // tn):
        o_ref[:, c * tn:(c + 1) * tn] = _dot(hb, w_ref[:, c * tn:(c + 1) * tn]).astype(o_ref.dtype)


def _in_proj(x2, nw, w_main, w_ab, l):
    tokens, d = x2.shape
    n = w_main.shape[2]
    tm = min(512, tokens)
    tn = min(2048, n)
    resident = pl.Buffered(1)
    return pl.pallas_call(
        functools.partial(_inproj_kernel, tn=tn),
        grid=(tokens // tm,),
        in_specs=[
            pl.BlockSpec((tm, d), lambda i: (i, 0)),
            _layer_spec((1, d), l),
            _layer_spec((d, n), l, pipeline_mode=resident),
            _layer_spec((d, LANES), l),
        ],
        out_specs=[
            pl.BlockSpec((tm, n), lambda i: (i, 0)),
            pl.BlockSpec((tm, LANES), lambda i: (i, 0)),
        ],
        out_shape=[
            jax.ShapeDtypeStruct((tokens, n), BF16),
            jax.ShapeDtypeStruct((tokens, LANES), F32),
        ],
        compiler_params=pltpu.CompilerParams(
            dimension_semantics=("parallel",), vmem_limit_bytes=VMEM_LIMIT),
        name="in_proj",
    )(x2, nw, w_main, w_ab)


def _bmm(a, b):
    return jnp.einsum("bmk,bkn->bmn", a, b, preferred_element_type=F32)


def _bmm_nt(a, b):
    return jnp.einsum("bmk,bnk->bmn", a, b, preferred_element_type=F32)


def _gdn_kernel(*refs, ts, kconv, n_t):
    t = pl.program_id(1)
    state = refs[9]

    @pl.when(t == 0)
    def _():
        state[...] = jnp.zeros_like(state)
        _gdn_region(*refs, ts=ts, kconv=kconv, do_scan=False, do_intra=True)

    @pl.when(jnp.logical_and(t > 0, t < n_t))
    def _():
        _gdn_region(*refs, ts=ts, kconv=kconv, do_scan=True, do_intra=True)

    @pl.when(t == n_t)
    def _():
        _gdn_region(*refs, ts=ts, kconv=kconv, do_scan=True, do_intra=False)


def _gdn_region(qkv_ref, halo_ref, gate_ref, ab_ref, convw_ref, alog_ref, dtb_ref, nw_ref, o_ref,
                state, u_buf, w_buf, aqk_buf, qdec_buf, kdec_buf, scale_buf,
                *, ts, kconv, do_scan, do_intra):
    t = pl.program_id(1)
    c_len = GDN_CHUNK
    dk = GDN_HEAD_DIM
    n_ch = ts // c_len

    nw = nw_ref[...]
    s = [state[h] for h in range(GDN_HEADS)]
    pend = {}

    def scan_first(c, h):
        i = h * n_ch + c
        s16 = s[h].astype(BF16)
        pend[h] = (u_buf[i] - _dot(w_buf[i], s16), _dot(qdec_buf[i], s16))

    def scan_second(c, h):
        rows = slice(c * c_len, (c + 1) * c_len)
        i = h * n_ch + c
        v_new, o_state = pend[h]
        vn16 = v_new.astype(BF16)
        o = o_state + _dot(aqk_buf[i], vn16)
        s[h] = s[h] * scale_buf[i, 0:1, :] + _dot(kdec_buf[i].T.astype(BF16), vn16)
        o = o * lax.rsqrt(jnp.mean(o * o, axis=-1, keepdims=True) + EPS) * nw
        gate = gate_ref[0, rows, h * dk:(h + 1) * dk].astype(F32)
        o_ref[0, rows, h * dk:(h + 1) * dk] = (o * (gate * _sigmoid(gate))).astype(o_ref.dtype)

    half_steps = [functools.partial(f, c, h) for c in range(n_ch) for f in (scan_first, scan_second)
                  for h in range(GDN_HEADS)] if do_scan else []

    def scan_advance(n=1):
        for _ in range(n):
            if half_steps:
                half_steps.pop(0)()

    def scan_finish():
        scan_advance(len(half_steps))
        if do_scan:
            for h in range(GDN_HEADS):
                state[h] = s[h]

    if not do_intra:
        scan_finish()
        return

    x_ext = jnp.concatenate([jnp.where(t == 0, 0.0, halo_ref[0].astype(F32)), qkv_ref[0].astype(F32)], axis=0)
    acc = None
    for i in range(kconv):
        shift = kconv - 1 - i
        xs = x_ext if shift == 0 else pltpu.roll(x_ext, shift, 0)
        term = convw_ref[i:i + 1, :] * xs[SUBLANES:, :]
        acc = term if acc is None else acc + term
    y = acc * _sigmoid(acc)

    ab = ab_ref[0]
    g_all = -jnp.exp(alog_ref[...]) * _softplus(ab + dtb_ref[...])
    beta_all = _sigmoid(ab)

    rt = lax.broadcasted_iota(jnp.int32, (ts, ts), 0)
    ct = lax.broadcasted_iota(jnp.int32, (ts, ts), 1)
    cum_mat = ((rt >= ct) & (rt // c_len == ct // c_len)).astype(BF16)
    g_hi = g_all.astype(BF16)
    g_r32 = g_all - g_hi.astype(F32)
    g_mid = g_r32.astype(BF16)
    g_lo = (g_r32 - g_mid.astype(F32)).astype(BF16)
    gcum = _dot(cum_mat, g_hi) + _dot(cum_mat, g_mid) + _dot(cum_mat, g_lo)

    def per_head(fn):
        return jnp.concatenate([fn(h) for h in range(GDN_HEADS)], axis=0)

    def head_cols(base):
        return per_head(lambda h: y[:, base + h * dk:base + (h + 1) * dk].reshape(n_ch, c_len, dk))

    q = head_cols(0)
    k = head_cols(GDN_WIDTH)
    v = head_cols(2 * GDN_WIDTH)
    q = q * lax.rsqrt(jnp.sum(q * q, axis=-1, keepdims=True) + EPS) * (dk ** -0.5)
    k = k * lax.rsqrt(jnp.sum(k * k, axis=-1, keepdims=True) + EPS)
    gc = per_head(lambda h: gcum[:, h:h + 1].reshape(n_ch, c_len, 1))
    beta = per_head(lambda h: beta_all[:, GDN_HEADS + h:GDN_HEADS + h + 1].reshape(n_ch, c_len, 1))
    gcum_t = [gcum[c * c_len:(c + 1) * c_len, :].T for c in range(n_ch)]
    gr = jnp.stack([gcum_t[c][h:h + 1, :] for h in range(GDN_HEADS) for c in range(n_ch)])
    g_end = gc[:, c_len - 1:c_len, :]

    row = lax.broadcasted_iota(jnp.int32, (c_len, c_len), 0)
    col = lax.broadcasted_iota(jnp.int32, (c_len, c_len), 1)
    tri = (row >= col)[None]
    strict = (row > col)[None]
    eye = (row == col).astype(F32)[None]

    decay = jnp.exp(jnp.where(tri, gc - gr, -1e30))
    egc = jnp.exp(gc)
    k_beta = k * beta
    k16 = k.astype(BF16)
    def bmm_s(a, b):
        scan_advance()
        return _bmm(a, b)

    scan_advance()
    a_kk = jnp.where(strict, _bmm_nt(k_beta.astype(BF16), k16) * decay, 0.0)
    p = -a_kk
    tinv = eye + p
    p_hi, p_lo = _split2(p)
    for _ in range(5):
        p = bmm_s(p_hi, p_hi) + (bmm_s(p_hi, p_lo) + bmm_s(p_lo, p_hi))
        p_hi, p_lo = _split2(p)
        t_hi, t_lo = _split2(tinv)
        tinv = tinv + (bmm_s(t_hi, p_hi) + (bmm_s(t_hi, p_lo) + bmm_s(t_lo, p_hi)))
    sol = bmm_s(tinv.astype(BF16), jnp.concatenate([v * beta, k_beta * egc], axis=-1).astype(BF16))
    a_qk = jnp.where(tri, _bmm_nt(q.astype(BF16), k16) * decay, 0.0)
    scan_finish()

    u_buf[...] = sol[:, :, :dk]
    w_buf[...] = sol[:, :, dk:].astype(BF16)
    aqk_buf[...] = a_qk.astype(BF16)
    qdec_buf[...] = (q * egc).astype(BF16)
    kdec_buf[...] = k * jnp.exp(g_end - gc)
    scale_buf[...] = jnp.broadcast_to(jnp.exp(g_end), scale_buf.shape)


def _gdn(proj3, ab3, conv_w, a_log_pad, dt_bias_pad, norm_w, l):
    bsz, seq, _ = proj3.shape
    kconv = conv_w.shape[1]
    ts = min(256, seq)
    qkv_w = 3 * GDN_WIDTH
    hb = ts // SUBLANES
    n_t = seq // ts
    nb = GDN_HEADS * (ts // GDN_CHUNK)
    dk = GDN_HEAD_DIM

    def ahead(colblk):
        return lambda b, t: (b, jnp.minimum(t, n_t - 1), colblk)

    def behind(colblk):
        return lambda b, t: (b, jnp.maximum(t - 1, 0), colblk)

    return pl.pallas_call(
        functools.partial(_gdn_kernel, ts=ts, kconv=kconv, n_t=n_t),
        grid=(bsz, n_t + 1),
        in_specs=[
            pl.BlockSpec((1, ts, qkv_w), ahead(COL_GDN_QKV // qkv_w)),
            pl.BlockSpec((1, SUBLANES, qkv_w),
                         lambda b, t: (b, jnp.maximum(jnp.minimum(t, n_t - 1) * hb - 1, 0), COL_GDN_QKV // qkv_w)),
            pl.BlockSpec((1, ts, GDN_WIDTH), behind(COL_GDN_GATE // GDN_WIDTH)),
            pl.BlockSpec((1, ts, LANES), ahead(0)),
            _layer_spec((kconv, qkv_w), l),
            _layer_spec((1, LANES), l),
            _layer_spec((1, LANES), l),
            _layer_spec((1, GDN_HEAD_DIM), l),
        ],
        out_specs=pl.BlockSpec((1, ts, GDN_WIDTH), behind(0)),
        out_shape=jax.ShapeDtypeStruct((bsz, seq, GDN_WIDTH), BF16),
        scratch_shapes=[
            pltpu.VMEM((GDN_HEADS, dk, dk), F32),
            pltpu.VMEM((nb, GDN_CHUNK, dk), F32),
            pltpu.VMEM((nb, GDN_CHUNK, dk), BF16),
            pltpu.VMEM((nb, GDN_CHUNK, GDN_CHUNK), BF16),
            pltpu.VMEM((nb, GDN_CHUNK, dk), BF16),
            pltpu.VMEM((nb, GDN_CHUNK, dk), F32),
            pltpu.VMEM((nb, SUBLANES, dk), F32),
        ],
        compiler_params=pltpu.CompilerParams(
            dimension_semantics=("parallel", "arbitrary"), vmem_limit_bytes=VMEM_LIMIT),
        name="gdn",
    )(proj3, proj3, proj3, ab3, conv_w, a_log_pad, dt_bias_pad, norm_w)


LOG2E = 1.4426950408889634
SB_EXIT = 104.0


def _softplus2(z2):
    neg_abs = lax.bitcast_convert_type(
        lax.bitcast_convert_type(z2, jnp.uint32) | jnp.uint32(0x80000000), F32)
    return jnp.maximum(z2, 0.0) + jnp.log2(1.0 + jnp.exp2(neg_abs))


def _sb_kernel(q_ref, k_ref, v_ref, o_ref, *, tq, nq):
    step = pl.program_id(2)
    pw = 2 * SB_HEAD_DIM
    first = lax.broadcasted_iota(jnp.int32, (tq, pw), 1) < SB_HEAD_DIM
    qsets = []
    for j in range(nq):
        q = q_ref[0, j * tq:(j + 1) * tq, :] * jnp.asarray(SB_HEAD_DIM ** -0.5, BF16)
        zero = jnp.zeros_like(q)
        qsets.append((jnp.where(first, q, zero), jnp.where(first, zero, q)))
    qbs = [step * nq + j for j in range(nq)]
    row = lax.broadcasted_iota(jnp.int32, (tq, tq), 0)
    col = lax.broadcasted_iota(jnp.int32, (tq, tq), 1)
    upper = (row > col).astype(BF16)
    upper2 = jnp.concatenate([upper, upper], axis=0)
    causal = col < row
    heads = range(2)

    def sweep(items, r_runs, accs):
        zs = [[_dot_nt(qh, k) * LOG2E for qh in qsets[j]] for j, (k, _), _, _ in items]
        sps = [[_softplus2(z) for z in zb] for zb in zs]
        sps = [[sp if it[2] is None else jnp.where(it[2], sp, 0.0) for sp in sb] for it, sb in zip(items, sps)]
        afters = [[_dot(jnp.concatenate(_split2(sp), axis=1), upper2) for sp in sb] for sb in sps]
        tots = [[jnp.sum(sp, axis=-1, keepdims=True) for sp in sb] for sb in sps]
        r_runs = {j: list(r) for j, r in r_runs.items()}
        accs = {j: list(a) for j, a in accs.items()}
        for b, (j, (_, v), mask, gate) in enumerate(items):
            for h in heads:
                att = jnp.exp2((zs[b][h] - sps[b][h]) - afters[b][h] - r_runs[j][h])
                if mask is not None:
                    att = jnp.where(mask, att, 0.0)
                pv = _dot(att.astype(BF16), v)
                tot = tots[b][h]
                if gate is not None:
                    pv = jnp.where(gate, pv, 0.0)
                    tot = jnp.where(gate, tot, 0.0)
                accs[j][h] = accs[j][h] + pv
                r_runs[j][h] = r_runs[j][h] + tot
        return r_runs, accs

    def kv(kb):
        k0 = pl.multiple_of(kb * tq, tq)
        return k_ref[0, pl.ds(k0, tq), :], v_ref[0, pl.ds(k0, tq), :]

    def done(r):
        return jnp.min(jnp.minimum(r[0], r[1]))

    blocks = {}
    for kb_off in range(nq, -2, -1):
        blocks[kb_off] = kv(jnp.maximum(step * nq + kb_off, 0)) if kb_off < nq else None
    items = []
    for j in range(nq):
        items.append((j, blocks[j], causal, None))
        items.append((j, blocks[j - 1], None, (step > 0) if j == 0 else None))
    zr = [jnp.zeros((tq, 1), F32)] * 2
    za = [jnp.zeros((tq, pw), F32)] * 2
    r_runs, accs = sweep(items, {j: zr for j in range(nq)}, {j: za for j in range(nq)})

    for j in range(nq):
        qb = qbs[j]

        def cond(c, qb=qb):
            return jnp.logical_and(c[0] <= qb, c[1] < SB_EXIT * LOG2E)

        def body(c, j=j, qb=qb):
            i, _, r, a = c
            r_new, a_new = sweep([(j, kv(qb - i), None, None)], {j: r}, {j: a})
            return i + 1, done(r_new[j]), r_new[j], a_new[j]

        _, _, _, a = lax.while_loop(cond, body, (jnp.int32(2), done(r_runs[j]), r_runs[j], accs[j]))
        o_ref[0, j * tq:(j + 1) * tq, :] = jnp.where(first, a[0], a[1]).astype(o_ref.dtype)


def _sb(proj3):
    bsz, seq, _ = proj3.shape
    tq = min(256, seq)
    nq = 4 if seq % (4 * tq) == 0 else 1
    pw = 2 * SB_HEAD_DIM
    pairs = SB_HEADS // 2
    return pl.pallas_call(
        functools.partial(_sb_kernel, tq=tq, nq=nq),
        grid=(bsz, pairs, seq // (nq * tq)),
        in_specs=[
            pl.BlockSpec((1, nq * tq, pw), lambda b, p, i: (b, i, COL_SB_Q // pw + p)),
            pl.BlockSpec((1, seq, pw), lambda b, p, i: (b, 0, COL_SB_K // pw + p)),
            pl.BlockSpec((1, seq, pw), lambda b, p, i: (b, 0, COL_SB_V // pw + p)),
        ],
        out_specs=pl.BlockSpec((1, nq * tq, pw), lambda b, p, i: (b, i, p)),
        out_shape=jax.ShapeDtypeStruct((bsz, seq, SB_WIDTH), BF16),
        compiler_params=pltpu.CompilerParams(
            dimension_semantics=("parallel", "parallel", "arbitrary"), vmem_limit_bytes=VMEM_LIMIT),
        name="stick_breaking",
    )(proj3, proj3, proj3)


def _merge_kernel(x_ref, ya_ref, yb_ref, scx_ref, scxh_ref, scc_ref, scch_ref, scb_ref,
                  g0_ref, g1_ref, g2_ref, convw_ref, wbr_ref, wout_ref, nw_ref, o_ref, cbuf,
                  *, tm, kconv):
    t = pl.program_id(1)
    halo = scxh_ref[0].astype(F32) * scch_ref[0].astype(F32)
    cbuf[0:SUBLANES, :] = jnp.where(t == 0, 0.0, halo)
    cbuf[SUBLANES:, :] = scx_ref[0].astype(F32) * scc_ref[0].astype(F32)
    conv = None
    for i in range(kconv):
        term = convw_ref[i:i + 1, :] * cbuf[pl.ds(SUBLANES - (kconv - 1) + i, tm), :]
        conv = term if conv is None else conv + term
    yc = scb_ref[0].astype(F32) * conv
    merged = (_sigmoid(g0_ref[0].astype(F32)) * _dot(ya_ref[0], wbr_ref[0])
              + _sigmoid(g1_ref[0].astype(F32)) * _dot(yb_ref[0], wbr_ref[1])
              + _sigmoid(g2_ref[0].astype(F32)) * _dot(yc.astype(BF16), wbr_ref[2]))
    r = _dot(merged.astype(BF16), wout_ref[...])
    o_ref[0] = x_ref[0] + _rms(r, nw_ref[...])


def _merge(x3, ya, yb, proj3, conv_w, w_br, w_out, nw, l):
    bsz, seq, d = x3.shape
    kconv = conv_w.shape[1]
    tm = min(512, seq)
    hb = tm // SUBLANES
    sw = SC_WIDTH

    def cur(colblk):
        return lambda b, t: (b, t, colblk)

    def prev(colblk):
        return lambda b, t: (b, jnp.maximum(t * hb - 1, 0), colblk)

    return pl.pallas_call(
        functools.partial(_merge_kernel, tm=tm, kconv=kconv),
        grid=(bsz, seq // tm),
        in_specs=[
            pl.BlockSpec((1, tm, d), cur(0)),
            pl.BlockSpec((1, tm, GDN_WIDTH), cur(0)),
            pl.BlockSpec((1, tm, SB_WIDTH), cur(0)),
            pl.BlockSpec((1, tm, sw), cur(COL_SC_X // sw)),
            pl.BlockSpec((1, SUBLANES, sw), prev(COL_SC_X // sw)),
            pl.BlockSpec((1, tm, sw), cur(COL_SC_C // sw)),
            pl.BlockSpec((1, SUBLANES, sw), prev(COL_SC_C // sw)),
            pl.BlockSpec((1, tm, sw), cur(COL_SC_B // sw)),
            pl.BlockSpec((1, tm, d), cur(COL_GATES // d)),
            pl.BlockSpec((1, tm, d), cur(COL_GATES // d + 1)),
            pl.BlockSpec((1, tm, d), cur(COL_GATES // d + 2)),
            _layer_spec((kconv, sw), l),
            _layer_spec((N_BRANCH, sw, d), l),
            _layer_spec((d, d), l),
            _layer_spec((1, d), l),
        ],
        out_specs=pl.BlockSpec((1, tm, d), cur(0)),
        out_shape=jax.ShapeDtypeStruct((bsz, seq, d), F32),
        scratch_shapes=[pltpu.VMEM((SUBLANES + tm, sw), F32)],
        compiler_params=pltpu.CompilerParams(
            dimension_semantics=("parallel", "arbitrary"), vmem_limit_bytes=VMEM_LIMIT),
        name="merge",
    )(x3, ya, yb, proj3, proj3, proj3, proj3, proj3, proj3, proj3, proj3, conv_w, w_br, w_out, nw)


def _ffn_kernel(x_ref, nw1_ref, w1_ref, w2_ref, nw2_ref, o_ref, *, tf):
    x = x_ref[...]
    h = _rms(x, nw1_ref[...]).astype(BF16)
    g = None
    for c in range(w1_ref.shape[1] // tf):
        f = jnp.maximum(_dot(h, w1_ref[:, c * tf:(c + 1) * tf]), 0.0)
        part = _dot((f * f).astype(BF16), w2_ref[c * tf:(c + 1) * tf, :])
        g = part if g is None else g + part
    o_ref[...] = x + _rms(g, nw2_ref[...])


def _ffn(x2, nw1, w1, w2, nw2, l):
    tokens, d = x2.shape
    dff = w1.shape[2]
    tm = min(512, tokens)
    tf = min(1024, dff)
    resident = pl.Buffered(1)
    return pl.pallas_call(
        functools.partial(_ffn_kernel, tf=tf),
        grid=(tokens // tm,),
        in_specs=[
            pl.BlockSpec((tm, d), lambda i: (i, 0)),
            _layer_spec((1, d), l),
            _layer_spec((d, dff), l, pipeline_mode=resident),
            _layer_spec((dff, d), l, pipeline_mode=resident),
            _layer_spec((1, d), l),
        ],
        out_specs=pl.BlockSpec((tm, d), lambda i: (i, 0)),
        out_shape=jax.ShapeDtypeStruct((tokens, d), F32),
        compiler_params=pltpu.CompilerParams(
            dimension_semantics=("parallel",), vmem_limit_bytes=VMEM_LIMIT),
        name="ffn",
    )(x2, nw1, w1, w2, nw2)


def _split_w_in(w):
    gw, sw = GDN_WIDTH, SB_WIDTH
    o = 0
    gdn_qkv = w[..., o:o + 3 * gw]; o += 3 * gw
    gdn_gate = w[..., o:o + gw]; o += gw
    ab = w[..., o:o + 2 * GDN_HEADS]; o += 2 * GDN_HEADS
    sb_qkv = w[..., o:o + 3 * sw]; o += 3 * sw
    sc_x = w[..., o:o + SC_WIDTH]; o += SC_WIDTH
    sc_b = w[..., o:o + SC_WIDTH]; o += SC_WIDTH
    sc_c = w[..., o:o + SC_WIDTH]; o += SC_WIDTH
    gates = w[..., o:]
    main = jnp.concatenate([gdn_qkv, gdn_gate, sb_qkv, sc_x, sc_c, sc_b, gates], axis=-1).astype(BF16)
    ab_pad = jnp.pad(ab, ((0, 0), (0, 0), (0, LANES - 2 * GDN_HEADS))).astype(BF16)
    return main, ab_pad


def _pad_lanes(v):
    return jnp.pad(v.astype(F32), ((0, 0), (0, LANES - v.shape[1])))[:, None, :]


def kernel(x, norm_mix_pre, w_in, conv_qkv_w, gdn_a_log, gdn_dt_bias, gdn_norm_w, conv_sc_w,
           w_branch, w_out, norm_mix_post, norm_ffn_pre, w_ff1, w_ff2, norm_ffn_post):
    bsz, seq, d = x.shape
    depth = w_in.shape[0]
    tokens = bsz * seq
    w_main, w_ab = _split_w_in(w_in)
    a_log_pad, dt_bias_pad = _pad_lanes(gdn_a_log), _pad_lanes(gdn_dt_bias)
    w_br16, w_out16, w_ff1_16, w_ff2_16 = (w.astype(BF16) for w in (w_branch, w_out, w_ff1, w_ff2))
    row = lambda v: v[:, None, :]
    for l in range(depth):
        proj, ab = _in_proj(x.reshape(tokens, d), row(norm_mix_pre), w_main, w_ab, l)
        proj3 = proj.reshape(bsz, seq, -1)
        y_a = _gdn(proj3, ab.reshape(bsz, seq, LANES), conv_qkv_w, a_log_pad, dt_bias_pad, row(gdn_norm_w), l)
        y_b = _sb(proj3)
        x = _merge(x, y_a, y_b, proj3, conv_sc_w, w_br16, w_out16, row(norm_mix_post), l)
        x = _ffn(x.reshape(tokens, d), row(norm_ffn_pre), w_ff1_16, w_ff2_16, row(norm_ffn_post),
                 l).reshape(bsz, seq, d)
    return x
```
